```python
import functools
import jax, jax.numpy as jnp
from jax import lax
import numpy as np

D_MODEL = 1024
BATCH = 16
SEQ = 2048
DEPTH = 4
DEC_BATCH = 128
DEC_SEQ = 8
PAST_LEN = 8192
PAGE_SIZE = 128

F32 = jnp.float32
EPS = 1e-6
N_EVEN = (DEPTH + 1) // 2
N_ODD = DEPTH // 2
GLA_HEADS = 4
GLA_DK = 64
GLA_DV = 128
GLA_RANK = 16
GLA_TAU = 16.0
GLA_CHUNK = 64
GLA_QK = GLA_HEADS * GLA_DK
GLA_VW = GLA_HEADS * GLA_DV
MLA_HEADS = 8
MLA_NOPE = 64
MLA_ROPE = 32
MLA_V = 64
MLA_QK = MLA_NOPE + MLA_ROPE
MLA_VW = MLA_HEADS * MLA_V
Q_LORA = 384
KV_LORA = 256
ROPE_THETA = 10000.0
Q_BLOCK = 128
EVEN_SPLITS = (GLA_QK, GLA_QK, GLA_VW, GLA_VW, GLA_RANK, Q_LORA, KV_LORA, MLA_ROPE, MLA_VW)
EVEN_IN = sum(EVEN_SPLITS)
EVEN_MIX = GLA_VW + MLA_VW
D_RNN = 1280
RNN_BLOCKS = 10
RNN_BW = D_RNN // RNN_BLOCKS
CONV_W = 4
LRU_C = 8.0

kernel_name = 'hybrid_gla_mla_rglru_step'


def rmsnorm(x, g):
    xf = x.astype(F32)
    y = xf * lax.rsqrt(jnp.mean(xf * xf, axis=-1, keepdims=True) + EPS)
    return (y * g.astype(F32)).astype(x.dtype)


def apply_rope(x, pos):
    half = MLA_ROPE // 2
    inv_freq = ROPE_THETA ** (-jnp.arange(half, dtype=F32) / half)
    ang = pos.astype(F32)[:, None] * inv_freq[None, :]
    cos = jnp.cos(ang)[None, :, None, :]
    sin = jnp.sin(ang)[None, :, None, :]
    xf = x.astype(F32)
    x1, x2 = xf[..., :half], xf[..., half:]
    return jnp.concatenate([x1 * cos - x2 * sin, x2 * cos + x1 * sin], axis=-1).astype(x.dtype)


def gla_scan(q, k, v, logf, s0):
    B, L = q.shape[:2]
    C = min(GLA_CHUNK, L)
    n = -(-L // C)
    pad = n * C - L
    def prep(t):
        t = t.astype(F32)
        if pad:
            t = jnp.pad(t, ((0, 0), (0, pad), (0, 0), (0, 0)))
        return t.reshape(B, n, C, t.shape[2], t.shape[3]).transpose(1, 0, 2, 3, 4)
    qs, ks, vs, gs = prep(q), prep(k), prep(v), prep(logf)
    mask = jnp.tril(jnp.ones((C, C), dtype=bool))[None, :, :, None, None]

    def step(S, inp):
        qc, kc, vc, gc = inp
        cum = jnp.cumsum(gc, axis=1)
        o_inter = jnp.einsum('bihk,bhkv->bihv', qc * jnp.exp(cum), S)
        diff = jnp.where(mask, cum[:, :, None] - cum[:, None, :], -jnp.inf)
        A = jnp.einsum('bihk,bjhk,bijhk->bhij', qc, kc, jnp.exp(diff))
        o_intra = jnp.einsum('bhij,bjhv->bihv', A, vc)
        total = cum[:, -1]
        kdec = kc * jnp.exp(total[:, None] - cum)
        S_new = jnp.exp(total)[..., None] * S + jnp.einsum('bjhk,bjhv->bhkv', kdec, vc)
        return S_new, o_inter + o_intra

    S, o = lax.scan(step, s0.astype(F32), (qs, ks, vs, gs))
    o = o.transpose(1, 0, 2, 3, 4).reshape(B, n * C, GLA_HEADS, GLA_DV)[:, :L]
    return o, S


def mla_queries(c_q, pos, q_norm, w_uq, qh_norm):
    B, T = c_q.shape[:2]
    q = (rmsnorm(c_q, q_norm) @ w_uq).reshape(B, T, MLA_HEADS, MLA_QK)
    q = rmsnorm(q, qh_norm)
    q = jnp.concatenate([q[..., :MLA_NOPE], apply_rope(q[..., MLA_NOPE:], pos)], axis=-1)
    return q * (MLA_QK ** -0.5)


def mla_keys(ckv_n, kr, pos, w_ukv, kh_norm):
    B, T = ckv_n.shape[:2]
    kv = (ckv_n @ w_ukv).reshape(B, T, MLA_HEADS, MLA_NOPE + MLA_V)
    k_nope, v = kv[..., :MLA_NOPE], kv[..., MLA_NOPE:]
    k = jnp.concatenate([k_nope, jnp.broadcast_to(kr[:, :, None, :], (B, T, MLA_HEADS, MLA_ROPE))], axis=-1)
    k = rmsnorm(k, kh_norm)
    k = jnp.concatenate([k[..., :MLA_NOPE], apply_rope(k[..., MLA_NOPE:], pos)], axis=-1)
    return k, v


def prompt_attention(q, k, v):
    B, T = q.shape[:2]
    qb_size = min(Q_BLOCK, T)
    nb = T // qb_size
    qb = q.reshape(B, nb, qb_size, MLA_HEADS, MLA_QK).transpose(1, 0, 2, 3, 4)
    kpos = jnp.arange(T)

    def blk(args):
        qi, bi = args
        s = jnp.einsum('bqhd,bkhd->bhqk', qi, k, preferred_element_type=F32)
        qpos = bi * qb_size + jnp.arange(qb_size)
        s = jnp.where(kpos[None, :] <= qpos[:, None], s, -jnp.inf)
        p = jax.nn.softmax(s, axis=-1)
        return jnp.einsum('bhqk,bkhd->bqhd', p.astype(v.dtype), v)

    o = lax.map(blk, (qb, jnp.arange(nb)))
    return o.transpose(1, 0, 2, 3, 4).reshape(B, T, MLA_HEADS, MLA_V)


def paged_attention(q, k_new, v_new, ckv_pool, kr_pool, layer, page_table, w_ukv, kh_norm):
    B, T = q.shape[:2]
    n_pages = page_table.shape[1]
    m0 = jnp.full((B, MLA_HEADS, T), -1e30, F32)
    l0 = jnp.zeros((B, MLA_HEADS, T), F32)
    acc0 = jnp.zeros((B, MLA_HEADS, T, MLA_V), F32)

    def update(carry, s, v):
        m, l, acc = carry
        m_new = jnp.maximum(m, s.max(-1))
        corr = jnp.exp(m - m_new)
        pe = jnp.exp(s - m_new[..., None])
        acc = acc * corr[..., None] + jnp.einsum('bhqk,bkhd->bhqd', pe, v.astype(F32))
        return (m_new, l * corr + pe.sum(-1), acc)

    def page_step(carry, p):
        phys = page_table[:, p]
        pos = p * PAGE_SIZE + jnp.arange(PAGE_SIZE)
        k, v = mla_keys(ckv_pool[layer, phys], kr_pool[layer, phys], pos, w_ukv, kh_norm)
        s = jnp.einsum('bqhd,bkhd->bhqk', q, k, preferred_element_type=F32)
        return update(carry, s, v), None

    carry, _ = lax.scan(page_step, (m0, l0, acc0), jnp.arange(n_pages))
    s = jnp.einsum('bqhd,bkhd->bhqk', q, k_new, preferred_element_type=F32)
    causal = jnp.tril(jnp.ones((T, T), dtype=bool))
    s = jnp.where(causal[None, None], s, -jnp.inf)
    m, l, acc = update(carry, s, v_new)
    return (acc / l[..., None]).transpose(0, 2, 1, 3).astype(q.dtype)


def even_layer(x, pos, gla_s0, attend, ln, w_in, w_f2, b_f, o_norm, q_norm, kv_norm, w_uq, w_ukv, qh_norm, kh_norm, w_out):
    B, T, _ = x.shape
    z = rmsnorm(x, ln) @ w_in
    q_a, k_a, v_a, g_a, f_a, c_q, c_kv, k_r, g_b = jnp.split(z, np.cumsum(EVEN_SPLITS)[:-1].tolist(), axis=-1)
    qa = q_a.reshape(B, T, GLA_HEADS, GLA_DK) * (GLA_DK ** -0.5)
    ka = k_a.reshape(B, T, GLA_HEADS, GLA_DK)
    va = v_a.reshape(B, T, GLA_HEADS, GLA_DV)
    logf = (jax.nn.log_sigmoid((f_a @ w_f2 + b_f).astype(F32)) / GLA_TAU).reshape(B, T, GLA_HEADS, GLA_DK)
    o_a, s_a = gla_scan(qa, ka, va, logf, gla_s0)
    o_a = rmsnorm(o_a.astype(x.dtype), o_norm).reshape(B, T, GLA_VW) * jax.nn.silu(g_a)
    q = mla_queries(c_q, pos, q_norm, w_uq, qh_norm)
    ckv_n = rmsnorm(c_kv, kv_norm)
    k, v = mla_keys(ckv_n, k_r, pos, w_ukv, kh_norm)
    o_b = attend(q, k, v).reshape(B, T, MLA_VW) * jax.nn.silu(g_b)
    y = jnp.concatenate([o_a, o_b], axis=-1) @ w_out
    return x + y, s_a.astype(x.dtype), ckv_n, k_r


def _lru_combine(c1, c2):
    a1, b1 = c1
    a2, b2 = c2
    return a1 * a2, a2 * b1 + b2


def odd_layer(x, h0, conv_buf, ln, w_in, conv_w, conv_b, w_a, b_a, w_x, b_x, lam, w_out):
    B, T, _ = x.shape
    u, gate = jnp.split(rmsnorm(x, ln) @ w_in, 2, axis=-1)
    full = jnp.concatenate([conv_buf.astype(u.dtype), u], axis=1)
    xc = conv_b
    for tap in range(CONV_W):
        xc = xc + full[:, tap:tap + T] * conv_w[tap]
    new_buf = full[:, T:]
    xb = xc.reshape(B, T, RNN_BLOCKS, RNN_BW)
    r = jax.nn.sigmoid((jnp.einsum('btnc,ncd->btnd', xb, w_a).reshape(B, T, D_RNN) + b_a).astype(F32))
    i = jax.nn.sigmoid((jnp.einsum('btnc,ncd->btnd', xb, w_x).reshape(B, T, D_RNN) + b_x).astype(F32))
    log_a = -LRU_C * r * jax.nn.softplus(-lam.astype(F32))
    a = jnp.exp(log_a)
    b = jnp.sqrt(-jnp.expm1(2.0 * log_a)) * (i * xc.astype(F32))
    b = b.at[:, 0].add(a[:, 0] * h0.astype(F32))
    _, h = lax.associative_scan(_lru_combine, (a, b), axis=1)
    y = (h.astype(x.dtype) * jax.nn.silu(gate)) @ w_out
    return x + y, h[:, -1].astype(x.dtype), new_buf


def setup_inputs(seed: int = 0) -> dict:
    key = jax.random.key(seed)
    ks = iter(jax.random.split(key, 40))
    def nrm(shape, scale):
        return jax.random.normal(next(ks), shape, F32) * scale
    def gain(shape):
        return 1.0 + nrm(shape, 0.02)
    n_pages = PAST_LEN // PAGE_SIZE
    n_phys = (5 * DEC_BATCH * n_pages) // 4
    ne, no = N_EVEN, N_ODD
    perm = jax.random.permutation(next(ks), n_phys)
    page_table = perm[: DEC_BATCH * n_pages].reshape(DEC_BATCH, n_pages).astype(jnp.int32)
    u = jax.random.uniform(next(ks), (no, D_RNN), F32, 0.9, 0.999)
    a0 = u ** (1.0 / LRU_C)
    lam = jnp.log(a0) - jnp.log1p(-a0)
    return {
        'x_prompt': nrm((BATCH, SEQ, D_MODEL), 1.0),
        'x_sample': nrm((DEC_BATCH, DEC_SEQ, D_MODEL), 1.0),
        'state_gla': nrm((ne, DEC_BATCH, GLA_HEADS, GLA_DK, GLA_DV), 1.0),
        'cache_ckv': nrm((ne, n_phys, PAGE_SIZE, KV_LORA), 1.0),
        'cache_kr': nrm((ne, n_phys, PAGE_SIZE, MLA_ROPE), 1.0),
        'state_rglru_h': nrm((no, DEC_BATCH, D_RNN), 0.5),
        'state_rglru_conv': nrm((no, DEC_BATCH, CONV_W - 1, D_RNN), 1.0),
        'page_table': page_table,
        'ln_even': gain((ne, D_MODEL)),
        'w_in_even': nrm((ne, D_MODEL, EVEN_IN), D_MODEL ** -0.5),
        'gla_w_f2': nrm((ne, GLA_RANK, GLA_QK), GLA_RANK ** -0.5),
        'gla_b_f': nrm((ne, GLA_QK), 0.1),
        'gla_out_norm': gain((ne, GLA_DV)),
        'mla_q_norm': gain((ne, Q_LORA)),
        'mla_kv_norm': gain((ne, KV_LORA)),
        'mla_w_uq': nrm((ne, Q_LORA, MLA_HEADS * MLA_QK), Q_LORA ** -0.5),
        'mla_w_ukv': nrm((ne, KV_LORA, MLA_HEADS * (MLA_NOPE + MLA_V)), KV_LORA ** -0.5),
        'mla_qh_norm': gain((ne, MLA_QK)),
        'mla_kh_norm': gain((ne, MLA_QK)),
        'w_out_even': nrm((ne, EVEN_MIX, D_MODEL), EVEN_MIX ** -0.5),
        'ln_odd': gain((no, D_MODEL)),
        'w_in_odd': nrm((no, D_MODEL, 2 * D_RNN), D_MODEL ** -0.5),
        'conv_w': nrm((no, CONV_W, D_RNN), CONV_W ** -0.5),
        'conv_b': nrm((no, D_RNN), 0.02),
        'rg_w_a': nrm((no, RNN_BLOCKS, RNN_BW, RNN_BW), RNN_BW ** -0.5),
        'rg_b_a': nrm((no, D_RNN), 0.02),
        'rg_w_x': nrm((no, RNN_BLOCKS, RNN_BW, RNN_BW), RNN_BW ** -0.5),
        'rg_b_x': nrm((no, D_RNN), 0.02),
        'rg_lambda': lam,
        'w_out_odd': nrm((no, D_RNN, D_MODEL), D_RNN ** -0.5),
    }


def reference(x_prompt, x_sample, state_gla, cache_ckv, cache_kr, state_rglru_h, state_rglru_conv, page_table,
              ln_even, w_in_even, gla_w_f2, gla_b_f, gla_out_norm, mla_q_norm, mla_kv_norm, mla_w_uq, mla_w_ukv,
              mla_qh_norm, mla_kh_norm, w_out_even, ln_odd, w_in_odd, conv_w, conv_b, rg_w_a, rg_b_a, rg_w_x,
              rg_b_x, rg_lambda, w_out_odd):
    past_len = page_table.shape[1] * PAGE_SIZE
    pos_p = jnp.arange(x_prompt.shape[1])
    pos_s = past_len + jnp.arange(x_sample.shape[1])
    bp = x_prompt.shape[0]
    yp, ys = x_prompt, x_sample
    gla_p, ckv_p, kr_p, rh_p, rc_p = [], [], [], [], []
    gla_s, ckv_s, kr_s, rh_s, rc_s = [], [], [], [], []
    for layer in range(DEPTH):
        j = layer // 2
        if layer % 2 == 0:
            ew = (ln_even[j], w_in_even[j], gla_w_f2[j], gla_b_f[j], gla_out_norm[j], mla_q_norm[j],
                  mla_kv_norm[j], mla_w_uq[j], mla_w_ukv[j], mla_qh_norm[j], mla_kh_norm[j], w_out_even[j])
            s0p = jnp.zeros((bp, GLA_HEADS, GLA_DK, GLA_DV), F32)
            yp, sp, cp, kp = even_layer(yp, pos_p, s0p, prompt_attention, *ew)
            attend_s = functools.partial(paged_attention, ckv_pool=cache_ckv, kr_pool=cache_kr, layer=j,
                                         page_table=page_table, w_ukv=mla_w_ukv[j], kh_norm=mla_kh_norm[j])
            ys, ss, cs, ksr = even_layer(ys, pos_s, state_gla[j], attend_s, *ew)
            gla_p.append(sp); ckv_p.append(cp); kr_p.append(kp)
            gla_s.append(ss); ckv_s.append(cs); kr_s.append(ksr)
        else:
            ow = (ln_odd[j], w_in_odd[j], conv_w[j], conv_b[j], rg_w_a[j], rg_b_a[j], rg_w_x[j], rg_b_x[j],
                  rg_lambda[j], w_out_odd[j])
            h0p = jnp.zeros((bp, D_RNN), yp.dtype)
            buf0p = jnp.zeros((bp, CONV_W - 1, D_RNN), yp.dtype)
            yp, hp, bufp = odd_layer(yp, h0p, buf0p, *ow)
            ys, hs, bufs = odd_layer(ys, state_rglru_h[j], state_rglru_conv[j], *ow)
            rh_p.append(hp); rc_p.append(bufp)
            rh_s.append(hs); rc_s.append(bufs)
    return (yp, ys,
            jnp.stack(gla_p), jnp.stack(ckv_p), jnp.stack(kr_p), jnp.stack(rh_p), jnp.stack(rc_p),
            jnp.stack(gla_s), jnp.stack(ckv_s), jnp.stack(kr_s), jnp.stack(rh_s), jnp.stack(rc_s))
```

```python
import functools

import jax
import jax.numpy as jnp
from jax import lax
from jax.experimental import pallas as pl
from jax.experimental.pallas import tpu as pltpu

F32 = jnp.float32
BF16 = jnp.bfloat16
EPS = 1e-6

GLA_HEADS = 4
GLA_DK = 64
GLA_DV = 128
GLA_RANK = 16
GLA_TAU = 16.0
GLA_CHUNK = 64
GLA_QK = GLA_HEADS * GLA_DK
GLA_VW = GLA_HEADS * GLA_DV
MLA_HEADS = 8
MLA_NOPE = 64
MLA_ROPE = 32
MLA_V = 64
MLA_QK = MLA_NOPE + MLA_ROPE
MLA_VW = MLA_HEADS * MLA_V
Q_LORA = 384
KV_LORA = 256
ROPE_THETA = 10000.0
PAGE_SIZE = 128
D_RNN = 1280
RNN_BLOCKS = 10
RNN_BW = D_RNN // RNN_BLOCKS
CONV_W = 4
LRU_C = 8.0

LANES = 128
SUBLANES = 8
HEAD_PAD = LANES
MLA_PADW = MLA_HEADS * HEAD_PAD
HALF_ROPE = MLA_ROPE // 2
ROPE_LANE0 = MLA_NOPE
MISC_W = LANES
EVEN_PACK = GLA_QK * 2 + GLA_VW * 2 + Q_LORA + KV_LORA + MLA_VW + MISC_W
EXP_CLAMP = 80.0
NEG_BIG = -1e30
VMEM_LIMIT = 48 * 1024 * 1024

NT_DIMS = (((1,), (1,)), ((), ()))
TN_DIMS = (((0,), (0,)), ((), ()))


def _dot(a, b):
    return jnp.dot(a, b, preferred_element_type=F32)


def _dot_nt(a, b):
    return lax.dot_general(a, b, NT_DIMS, preferred_element_type=F32)


def _dot_tn(a, b):
    return lax.dot_general(a, b, TN_DIMS, preferred_element_type=F32)


def _rms(x, g, n=None):
    n = x.shape[-1] if n is None else n
    ss = jnp.sum(x * x, axis=-1, keepdims=True) * (1.0 / n)
    return x * lax.rsqrt(ss + EPS) * g


def _silu(x):
    return x * jax.nn.sigmoid(x)


def _log_sigmoid(x):
    return jnp.minimum(x, 0.0) - jnp.log1p(jnp.exp(-jnp.abs(x)))


def _rope_tile(t, c, sp, sm):
    return t * c + pltpu.roll(t, HALF_ROPE, 1) * sp + pltpu.roll(t, LANES - HALF_ROPE, 1) * sm


def _params(sem):
    return pltpu.CompilerParams(dimension_semantics=sem, vmem_limit_bytes=VMEM_LIMIT)


def _full(shape):
    nd = len(shape)
    return pl.BlockSpec(shape, lambda *_: (0,) * nd)


def _even_front_kernel(x_ref, ln_ref, win_ref, wf2_ref, bf_ref, qn_ref, kvn_ref, wuq_ref, wuk_ref,
                       wuv_ref, qh_ref, kh_ref, c_ref, sp_ref, sm_ref,
                       qa_ref, ka_ref, va_ref, lf_ref, ga_ref, gb_ref, q_ref, k_ref, v_ref,
                       ckv_ref, misc_ref):
    x = x_ref[...]
    xn = _rms(x, ln_ref[...])
    z = _dot(xn.astype(BF16), win_ref[...])
    o = 0
    qa_ref[...] = z[:, o:o + GLA_QK] * (GLA_DK ** -0.5); o += GLA_QK
    ka_ref[...] = z[:, o:o + GLA_QK]; o += GLA_QK
    va_ref[...] = z[:, o:o + GLA_VW]; o += GLA_VW
    ga_ref[...] = _silu(z[:, o:o + GLA_VW]); o += GLA_VW
    cq = z[:, o:o + Q_LORA]; o += Q_LORA
    ckv = z[:, o:o + KV_LORA]; o += KV_LORA
    gb_ref[...] = _silu(z[:, o:o + MLA_VW]); o += MLA_VW
    misc = z[:, o:o + MISC_W]
    misc_ref[...] = misc

    f = _dot(misc.astype(BF16), wf2_ref[...]) + bf_ref[...]
    lf_ref[...] = _log_sigmoid(f) * (1.0 / GLA_TAU)

    c = c_ref[...]
    sp = sp_ref[...]
    sm = sm_ref[...]

    cqn = _rms(cq, qn_ref[...])
    qraw = _dot(cqn.astype(BF16), wuq_ref[...])
    qh = qh_ref[...]
    for h in range(MLA_HEADS):
        t = qraw[:, h * HEAD_PAD:(h + 1) * HEAD_PAD]
        t = _rope_tile(_rms(t, qh, MLA_QK), c, sp, sm) * (MLA_QK ** -0.5)
        q_ref[:, h * HEAD_PAD:(h + 1) * HEAD_PAD] = t.astype(BF16)

    ckvn = _rms(ckv, kvn_ref[...])
    ckv_ref[...] = ckvn
    cb = ckvn.astype(BF16)
    kraw = _dot(cb, wuk_ref[...])
    lane = lax.broadcasted_iota(jnp.int32, misc.shape, 1)
    krt = jnp.where(lane >= ROPE_LANE0, misc, 0.0)
    kh = kh_ref[...]
    for h in range(MLA_HEADS):
        t = kraw[:, h * HEAD_PAD:(h + 1) * HEAD_PAD] + krt
        t = _rope_tile(_rms(t, kh, MLA_QK), c, sp, sm)
        k_ref[:, h * HEAD_PAD:(h + 1) * HEAD_PAD] = t.astype(BF16)
    v_ref[...] = _dot(cb, wuv_ref[...]).astype(BF16)


def _even_front(x2, w, tabs, tm, rows_per_seq):
    n, d = x2.shape
    nblk_per_seq = rows_per_seq // tm if rows_per_seq >= tm else None
    row = lambda w_: pl.BlockSpec((tm, w_), lambda i: (i, 0))
    if nblk_per_seq is not None:
        tab = pl.BlockSpec((tm, LANES), lambda i: (i % nblk_per_seq, 0))
    else:
        tab = pl.BlockSpec((tm, LANES), lambda i: (0, 0))
    ins = [x2, w['ln'], w['win'], w['wf2'], w['bf'], w['qn'], w['kvn'], w['wuq'], w['wukp'],
           w['wuv'], w['qh'], w['kh'], tabs[0], tabs[1], tabs[2]]
    in_specs = [row(d)] + [_full(a.shape) for a in ins[1:12]] + [tab, tab, tab]
    outs = [(GLA_QK, F32), (GLA_QK, F32), (GLA_VW, F32), (GLA_QK, F32), (GLA_VW, F32), (MLA_VW, F32),
            (MLA_PADW, BF16), (MLA_PADW, BF16), (MLA_VW, BF16), (KV_LORA, F32), (MISC_W, F32)]
    return pl.pallas_call(
        _even_front_kernel,
        grid=(n // tm,),
        in_specs=in_specs,
        out_specs=[row(w_) for w_, _ in outs],
        out_shape=[jax.ShapeDtypeStruct((n, w_), dt) for w_, dt in outs],
        compiler_params=_params(("parallel",)),
        name="even_front",
    )(*ins)


def _gla_kernel(*refs, chunk, n_chunks, has_s0):
    if has_s0:
        q_ref, k_ref, v_ref, g_ref, s0_ref, o_ref, st_ref, s_scr = refs
    else:
        q_ref, k_ref, v_ref, g_ref, o_ref, st_ref, s_scr = refs
    tb = pl.program_id(1)

    @pl.when(tb == 0)
    def _():
        if has_s0:
            s_scr[...] = s0_ref[0]
        else:
            s_scr[...] = jnp.zeros_like(s_scr)

    c_ = chunk
    row = lax.broadcasted_iota(jnp.int32, (c_, c_), 0)
    col = lax.broadcasted_iota(jnp.int32, (c_, c_), 1)
    tri = row >= col
    tri_bf = jnp.where(tri, 1.0, 0.0).astype(BF16)
    mid_row = c_ // 2 - 1
    for c in range(n_chunks):
        sl = slice(c * c_, (c + 1) * c_)
        g = g_ref[0, sl, :]
        g_hi = g.astype(BF16)
        g_lo = (g - g_hi.astype(F32)).astype(BF16)
        cum = _dot(tri_bf, g_hi) + _dot(tri_bf, g_lo)
        tot = cum[c_ - 1:c_, :]
        mid = cum[mid_row:mid_row + 1, :]
        q = q_ref[0, sl, :]
        k = k_ref[0, sl, :]
        qd = (q * jnp.exp(cum)).astype(BF16)
        qm = (q * jnp.exp(jnp.minimum(cum - mid, EXP_CLAMP))).astype(BF16)
        km = (k * jnp.exp(jnp.minimum(mid - cum, EXP_CLAMP))).astype(BF16)
        kd = (k * jnp.exp(tot - cum)).astype(BF16)
        etot = jnp.exp(tot)
        for h in range(GLA_HEADS):
            hs = slice(h * GLA_DK, (h + 1) * GLA_DK)
            vs = slice(h * GLA_DV, (h + 1) * GLA_DV)
            a = _dot_nt(qm[:, hs], km[:, hs])
            a = jnp.where(tri, a, 0.0).astype(BF16)
            vh = v_ref[0, sl, vs].astype(BF16)
            st = s_scr[h]
            o_ref[0, sl, vs] = _dot_nt(qd[:, hs], st.astype(BF16)) + _dot(a, vh)
            s_scr[h] = st * etot[:, hs] + _dot_tn(vh, kd[:, hs])

    @pl.when(tb == pl.num_programs(1) - 1)
    def _():
        st_ref[0] = s_scr[...]


def _gla_scan(qa, ka, va, lf, s0t, chunk, tblk):
    b, t, _ = qa.shape
    n_chunks = tblk // chunk
    has_s0 = s0t is not None
    seq = lambda w_: pl.BlockSpec((1, tblk, w_), lambda i, j: (i, j, 0))
    st_spec = pl.BlockSpec((1, GLA_HEADS, GLA_DV, GLA_DK), lambda i, j: (i, 0, 0, 0))
    ins = [qa, ka, va, lf] + ([s0t] if has_s0 else [])
    in_specs = [seq(GLA_QK), seq(GLA_QK), seq(GLA_VW), seq(GLA_QK)] + ([st_spec] if has_s0 else [])
    return pl.pallas_call(
        functools.partial(_gla_kernel, chunk=chunk, n_chunks=n_chunks, has_s0=has_s0),
        grid=(b, t // tblk),
        in_specs=in_specs,
        out_specs=[seq(GLA_VW), st_spec],
        out_shape=[jax.ShapeDtypeStruct((b, t, GLA_VW), F32),
                   jax.ShapeDtypeStruct((b, GLA_HEADS, GLA_DV, GLA_DK), F32)],
        scratch_shapes=[pltpu.VMEM((GLA_HEADS, GLA_DV, GLA_DK), F32)],
        compiler_params=_params(("parallel", "arbitrary")),
        name="gla_scan",
    )(*ins)


def _flash_kernel(q_ref, k_ref, v_ref, o_ref, m_scr, l_scr, acc_scr, *, tq, tk):
    qi = pl.program_id(2)
    ki = pl.program_id(3)

    @pl.when(ki == 0)
    def _():
        m_scr[...] = jnp.full_like(m_scr, NEG_BIG)
        l_scr[...] = jnp.zeros_like(l_scr)
        acc_scr[...] = jnp.zeros_like(acc_scr)

    @pl.when(ki <= qi)
    def _():
        row = lax.broadcasted_iota(jnp.int32, (tq, tk), 0) + qi * tq
        col = lax.broadcasted_iota(jnp.int32, (tq, tk), 1) + ki * tk
        keep = col <= row
        v = v_ref[0]
        lane = lax.broadcasted_iota(jnp.int32, (tq, LANES), 1)
        acc = acc_scr[...]
        for hh in range(2):
            qh = q_ref[0, :, hh * HEAD_PAD:(hh + 1) * HEAD_PAD]
            kh = k_ref[0, :, hh * HEAD_PAD:(hh + 1) * HEAD_PAD]
            s = jnp.where(keep, _dot_nt(qh, kh), NEG_BIG)
            m_prev = m_scr[hh]
            m_new = jnp.maximum(m_prev, jnp.max(s, axis=-1, keepdims=True))
            alpha = jnp.exp(m_prev - m_new)
            p = jnp.exp(s - m_new[:, 0:1])
            l_scr[hh] = alpha * l_scr[hh] + jnp.sum(p, axis=-1, keepdims=True)
            m_scr[hh] = m_new
            pv = _dot(p.astype(BF16), v)
            mine = (lane >= hh * MLA_V) & (lane < (hh + 1) * MLA_V)
            acc = jnp.where(mine, alpha * acc + pv, acc)
        acc_scr[...] = acc

    @pl.when(ki == pl.num_programs(3) - 1)
    def _():
        lane = lax.broadcasted_iota(jnp.int32, (tq, LANES), 1)
        l = jnp.where(lane < MLA_V, l_scr[0], l_scr[1])
        o_ref[0] = acc_scr[...] / l


def _flash_attn(q, k, v, tq):
    b, t, _ = q.shape
    tk = tq
    nq = t // tq
    hp = MLA_HEADS // 2
    return pl.pallas_call(
        functools.partial(_flash_kernel, tq=tq, tk=tk),
        grid=(b, hp, nq, nq),
        in_specs=[pl.BlockSpec((1, tq, 2 * HEAD_PAD), lambda bi, h, i, j: (bi, i, h)),
                  pl.BlockSpec((1, tk, 2 * HEAD_PAD), lambda bi, h, i, j: (bi, jnp.minimum(i, j), h)),
                  pl.BlockSpec((1, tk, 2 * MLA_V), lambda bi, h, i, j: (bi, jnp.minimum(i, j), h))],
        out_specs=pl.BlockSpec((1, tq, 2 * MLA_V), lambda bi, h, i, j: (bi, i, h)),
        out_shape=jax.ShapeDtypeStruct((b, t, MLA_VW), F32),
        scratch_shapes=[pltpu.VMEM((2, tq, LANES), F32), pltpu.VMEM((2, tq, LANES), F32),
                        pltpu.VMEM((tq, LANES), F32)],
        compiler_params=_params(("parallel", "parallel", "parallel", "arbitrary")),
        name="flash_attn",
    )(q, k, v)


def _paged_kernel(pt_ref, q_ref, qr_ref, qrot_ref, latn_ref, krn_ref, cosn_ref, sinn_ref, cos_ref, sin_ref,
                  khg_ref, gr_ref, wukp_ref, wuk_ref, wuv_ref, *rest, n_pg, t_new):
    lat_refs = rest[:n_pg]
    kr_refs = rest[n_pg:2 * n_pg]
    o_ref, ql_scr, m_scr, l_scr, acc_scr = rest[2 * n_pg:]
    p_idx = pl.program_id(1)
    nq = MLA_HEADS * t_new

    @pl.when(p_idx == 0)
    def _():
        qt = q_ref[0].astype(F32)
        qrep = jnp.concatenate([qt] * MLA_HEADS, axis=0)
        r_ = lax.broadcasted_iota(jnp.int32, qrep.shape, 0) // t_new
        c_ = lax.broadcasted_iota(jnp.int32, qrep.shape, 1) // HEAD_PAD
        qbd = jnp.where(r_ == c_, qrep * khg_ref[...], 0.0)
        ql_scr[...] = _dot_nt(qbd.astype(BF16), wukp_ref[...]).astype(BF16)
        m_scr[...] = jnp.full_like(m_scr, NEG_BIG)
        l_scr[...] = jnp.zeros_like(l_scr)
        acc_scr[...] = jnp.zeros_like(acc_scr)

    er = lax.broadcasted_iota(jnp.int32, (nq, MLA_HEADS * MLA_NOPE), 0) // t_new
    ec = lax.broadcasted_iota(jnp.int32, (nq, MLA_HEADS * MLA_NOPE), 1) // MLA_NOPE
    head_sel = er == ec
    e_bf = jnp.where(head_sel, 1.0, 0.0).astype(BF16)
    ones_bf = jnp.ones((nq, MLA_ROPE), BF16)
    gr = gr_ref[...]
    qr = qr_ref[0]
    qrot = qrot_ref[0]

    def attend(lat, kr, cos, sin, mask):
        cb = lat.astype(BF16)
        kn = _dot(cb, wuk_ref[...])
        ssq = _dot_nt(e_bf, (kn * kn).astype(BF16)) + _dot_nt(ones_bf, (kr * kr).astype(BF16))
        rinv = lax.rsqrt(ssq * (1.0 / MLA_QK) + EPS)
        x = kr * gr
        s = (_dot_nt(ql_scr[...], cb) + _dot_nt(qr, (x * cos).astype(BF16))
             + _dot_nt(qrot, (x * sin).astype(BF16))) * rinv
        if mask is not None:
            s = jnp.where(mask, s, NEG_BIG)
        m_prev = m_scr[...]
        m_new = jnp.maximum(m_prev, jnp.max(s, axis=-1, keepdims=True))
        alpha = jnp.exp(m_prev - m_new)
        p = jnp.exp(s - m_new[:, 0:1])
        l_scr[...] = alpha * l_scr[...] + jnp.sum(p, axis=-1, keepdims=True)
        m_scr[...] = m_new
        acc_scr[...] = alpha[:, 0:1] * acc_scr[...] + _dot(p.astype(BF16), cb)

    for g in range(n_pg):
        ps = slice(g * PAGE_SIZE, (g + 1) * PAGE_SIZE)
        attend(lat_refs[g][0, 0], kr_refs[g][0, 0], cos_ref[ps, :], sin_ref[ps, :], None)

    @pl.when(p_idx == pl.num_programs(1) - 1)
    def _():
        qi = lax.broadcasted_iota(jnp.int32, (nq, t_new), 0) % t_new
        kj = lax.broadcasted_iota(jnp.int32, (nq, t_new), 1)
        attend(latn_ref[0], krn_ref[0], cosn_ref[...], sinn_ref[...], kj <= qi)
        out_lat = acc_scr[...] / l_scr[:, 0:1]
        full = jnp.where(head_sel, _dot(out_lat.astype(BF16), wuv_ref[...]), 0.0)
        o = full[0:t_new]
        for h in range(1, MLA_HEADS):
            o = o + full[h * t_new:(h + 1) * t_new]
        o_ref[0] = o


def _paged_attn(layer, page_table_flat, n_pages, q3, qr, qrot, latn, krn, tabs_new, tabs_past, w,
                cache_ckv, cache_kr, n_pg):
    b, t_new, _ = q3.shape
    nq = MLA_HEADS * t_new
    steps = n_pages // n_pg
    tt = n_pg * PAGE_SIZE

    def per_b(shape):
        nd = len(shape)
        return pl.BlockSpec((1,) + shape[1:], lambda bi, p, pt: (bi,) + (0,) * (nd - 1))

    def const(shape):
        nd = len(shape)
        return pl.BlockSpec(shape, lambda bi, p, pt: (0,) * nd)

    def page_spec(width, g):
        return pl.BlockSpec((1, 1, PAGE_SIZE, width),
                            lambda bi, p, pt: (layer, pt[bi * n_pages + p * n_pg + g], 0, 0))

    past = pl.BlockSpec((tt, MLA_ROPE), lambda bi, p, pt: (p, 0))
    ins = [q3, qr, qrot, latn, krn, tabs_new[0], tabs_new[1], tabs_past[0], tabs_past[1],
           w['khg'], w['gr'], w['wukp'], w['wuk'], w['wuv']]
    in_specs = ([per_b(a.shape) for a in ins[:5]] + [const(ins[5].shape), const(ins[6].shape), past, past]
                + [const(a.shape) for a in ins[9:]])
    ins += [cache_ckv] * n_pg + [cache_kr] * n_pg
    in_specs += [page_spec(KV_LORA, g) for g in range(n_pg)] + [page_spec(MLA_ROPE, g) for g in range(n_pg)]
    grid_spec = pltpu.PrefetchScalarGridSpec(
        num_scalar_prefetch=1,
        grid=(b, steps),
        in_specs=in_specs,
        out_specs=pl.BlockSpec((1, t_new, MLA_VW), lambda bi, p, pt: (bi, 0, 0)),
        scratch_shapes=[pltpu.VMEM((nq, KV_LORA), BF16), pltpu.VMEM((nq, LANES), F32),
                        pltpu.VMEM((nq, LANES), F32), pltpu.VMEM((nq, KV_LORA), F32)],
    )
    return pl.pallas_call(
        functools.partial(_paged_kernel, n_pg=n_pg, t_new=t_new),
        grid_spec=grid_spec,
        out_shape=jax.ShapeDtypeStruct((b, t_new, MLA_VW), F32),
        compiler_params=_params(("parallel", "arbitrary")),
        name="paged_attn",
    )(page_table_flat, *ins)


def _even_back_kernel(x_ref, oa_ref, ga_ref, ob_ref, gb_ref, on_ref, wa_ref, wb_ref, y_ref):
    oa = oa_ref[...]
    on = on_ref[...]
    parts = []
    for h in range(GLA_HEADS):
        parts.append(_rms(oa[:, h * GLA_DV:(h + 1) * GLA_DV], on))
    a = jnp.concatenate(parts, axis=-1) * ga_ref[...]
    b_ = ob_ref[...] * gb_ref[...]
    y_ref[...] = x_ref[...] + _dot(a.astype(BF16), wa_ref[...]) + _dot(b_.astype(BF16), wb_ref[...])


def _even_back(x2, oa, ga, ob, gb, w, tm):
    n, d = x2.shape
    row = lambda w_: pl.BlockSpec((tm, w_), lambda i: (i, 0))
    ins = [x2, oa, ga, ob, gb, w['on'], w['wout_a'], w['wout_b']]
    return pl.pallas_call(
        _even_back_kernel,
        grid=(n // tm,),
        in_specs=[row(d), row(GLA_VW), row(GLA_VW), row(MLA_VW), row(MLA_VW)] + [_full(a.shape) for a in ins[5:]],
        out_specs=row(d),
        out_shape=jax.ShapeDtypeStruct((n, d), F32),
        compiler_params=_params(("parallel",)),
        name="even_back",
    )(*ins)


def _group_scan(a, b):
    t = lax.broadcasted_iota(jnp.int32, a.shape, 0) % SUBLANES
    for s in (1, 2, 4):
        keep = t >= s
        ar = pltpu.roll(a, s, 0)
        br = pltpu.roll(b, s, 0)
        b = jnp.where(keep, a * br + b, b)
        a = jnp.where(keep, a * ar, a)
    return a, b


def _odd_gates(xc, wax_ref, ba_ref, bx_ref, lam_ref):
    xcb = xc.astype(BF16)
    rs, is_ = [], []
    for n in range(RNN_BLOCKS):
        ga = _dot(xcb[:, n * RNN_BW:(n + 1) * RNN_BW], wax_ref[n])
        rs.append(ga[:, :RNN_BW])
        is_.append(ga[:, RNN_BW:])
    r = jax.nn.sigmoid(jnp.concatenate(rs, axis=-1) + ba_ref[...])
    i_ = jax.nn.sigmoid(jnp.concatenate(is_, axis=-1) + bx_ref[...])
    lam = lam_ref[...]
    sp_neg_lam = jnp.maximum(-lam, 0.0) + jnp.log1p(jnp.exp(-jnp.abs(lam)))
    log_a = (-LRU_C) * r * sp_neg_lam
    a = jnp.exp(log_a)
    b = jnp.sqrt(1.0 - jnp.exp(2.0 * log_a)) * (i_ * xc)
    return a, b


def _odd_prompt_kernel(x_ref, ln_ref, wu_ref, wg_ref, cw_ref, cb_ref, wax_ref, ba_ref, bx_ref, lam_ref,
                       wout_ref, y_ref, hl_ref, cv_ref, ush_scr, hc_scr, *, tt):
    tb = pl.program_id(1)
    ns = SUBLANES

    @pl.when(tb == 0)
    def _():
        ush_scr[0:ns, :] = jnp.zeros((ns, D_RNN), F32)
        hc_scr[...] = jnp.zeros_like(hc_scr)

    x = x_ref[0]
    xb = _rms(x, ln_ref[...]).astype(BF16)
    u = _dot(xb, wu_ref[...])
    gate = _dot(xb, wg_ref[...])
    ush_scr[ns:ns + tt, :] = u
    cw = cw_ref[...]
    xc = cb_ref[...] + u * cw[CONV_W - 1:CONV_W, :]
    for s in range(1, CONV_W):
        xc = xc + ush_scr[ns - s:ns - s + tt, :] * cw[CONV_W - 1 - s:CONV_W - s, :]
    ush_scr[0:ns, :] = ush_scr[tt:tt + ns, :]
    a, b = _odd_gates(xc, wax_ref, ba_ref, bx_ref, lam_ref)
    a, b = _group_scan(a, b)
    c = hc_scr[...]
    hs = []
    for g in range(tt // ns):
        ag = a[g * ns:(g + 1) * ns]
        hg = ag * c + b[g * ns:(g + 1) * ns]
        hs.append(hg)
        c = hg[ns - 1:ns, :]
    hc_scr[...] = c
    h = jnp.concatenate(hs, axis=0)
    y_ref[0] = x + _dot((h * _silu(gate)).astype(BF16), wout_ref[...])

    @pl.when(tb == pl.num_programs(1) - 1)
    def _():
        hl_ref[0] = c
        cv_ref[0] = ush_scr[ns - (CONV_W - 1):ns, :]


def _odd_sample_kernel(x_ref, ext_ref, ln_ref, wu_ref, wg_ref, cw_ref, cb_ref, wax_ref, ba_ref, bx_ref, lam_ref,
                       wout_ref, y_ref, h_ref, u_ref):
    ns = SUBLANES
    x = x_ref[...]
    rows = x.shape[0]
    xb = _rms(x, ln_ref[...]).astype(BF16)
    u = _dot(xb, wu_ref[...])
    gate = _dot(xb, wg_ref[...])
    u_ref[...] = u
    ext = ext_ref[...]
    t = lax.broadcasted_iota(jnp.int32, (rows, D_RNN), 0) % ns
    cw = cw_ref[...]
    xc = cb_ref[...] + u * cw[CONV_W - 1:CONV_W, :]
    for s in range(1, CONV_W):
        prev = jnp.where(t < s, pltpu.roll(ext, rows + s - ns, 0), pltpu.roll(u, s, 0))
        xc = xc + prev * cw[CONV_W - 1 - s:CONV_W - s, :]
    a, b = _odd_gates(xc, wax_ref, ba_ref, bx_ref, lam_ref)
    h0_at0 = jnp.where(t == 0, pltpu.roll(ext, rows - (ns - CONV_W), 0), 0.0)
    _, h = _group_scan(a, b + a * h0_at0)
    h_ref[...] = h
    y_ref[...] = x + _dot((h * _silu(gate)).astype(BF16), wout_ref[...])


def _odd_prompt(x, w, tt):
    b, t, d = x.shape
    wl = [w['ln'], w['wu'], w['wg'], w['cw'], w['cb'], w['wax'], w['ba'], w['bx'], w['lam'], w['wout']]
    return pl.pallas_call(
        functools.partial(_odd_prompt_kernel, tt=tt),
        grid=(b, t // tt),
        in_specs=[pl.BlockSpec((1, tt, d), lambda i, j: (i, j, 0))] + [_full(a.shape) for a in wl],
        out_specs=[pl.BlockSpec((1, tt, d), lambda i, j: (i, j, 0)),
                   pl.BlockSpec((1, 1, D_RNN), lambda i, j: (i, 0, 0)),
                   pl.BlockSpec((1, CONV_W - 1, D_RNN), lambda i, j: (i, 0, 0))],
        out_shape=[jax.ShapeDtypeStruct((b, t, d), F32), jax.ShapeDtypeStruct((b, 1, D_RNN), F32),
                   jax.ShapeDtypeStruct((b, CONV_W - 1, D_RNN), F32)],
        scratch_shapes=[pltpu.VMEM((tt + SUBLANES, D_RNN), F32), pltpu.VMEM((1, D_RNN), F32)],
        compiler_params=_params(("parallel", "arbitrary")),
        name="odd_prompt",
    )(x, *wl)


def _odd_sample(x2, ext, w, rows):
    n, d = x2.shape
    wl = [w['ln'], w['wu'], w['wg'], w['cw'], w['cb'], w['wax'], w['ba'], w['bx'], w['lam'], w['wout']]
    row = lambda w_: pl.BlockSpec((rows, w_), lambda i: (i, 0))
    return pl.pallas_call(
        _odd_sample_kernel,
        grid=(n // rows,),
        in_specs=[row(d), row(D_RNN)] + [_full(a.shape) for a in wl],
        out_specs=[row(d), row(D_RNN), row(D_RNN)],
        out_shape=[jax.ShapeDtypeStruct((n, d), F32), jax.ShapeDtypeStruct((n, D_RNN), F32),
                   jax.ShapeDtypeStruct((n, D_RNN), F32)],
        compiler_params=_params(("parallel",)),
        name="odd_sample",
    )(x2, ext, *wl)


def _odd_kernel_unused(*refs, n_grp, is_prompt):
    if is_prompt:
        (x_ref, ln_ref, wu_ref, wg_ref, cw_ref, cb_ref, wax_ref, ba_ref, bx_ref, lam_ref, wout_ref,
         y_ref, hl_ref, cv_ref, sh_scr, cin_scr, hc_scr) = refs
    else:
        (x_ref, h0_ref, cbuf_ref, ln_ref, wu_ref, wg_ref, cw_ref, cb_ref, wax_ref, ba_ref, bx_ref, lam_ref,
         wout_ref, y_ref, hl_ref, cv_ref) = refs
    g_ = n_grp
    ns = SUBLANES
    tb = pl.program_id(1) if is_prompt else None

    def xrows(k):
        if is_prompt:
            return x_ref[0, pl.ds(k, g_, stride=ns), :]
        return x_ref[pl.ds(k, g_, stride=ns), :]

    xp = jnp.concatenate([xrows(k) for k in range(ns)], axis=0)
    xb = _rms(xp, ln_ref[...]).astype(BF16)
    u = _dot(xb, wu_ref[...])
    gate = _dot(xb, wg_ref[...])
    us = [u[k * g_:(k + 1) * g_] for k in range(ns)]

    ntap = CONV_W - 1
    if is_prompt:
        @pl.when(tb == 0)
        def _():
            sh_scr[...] = jnp.zeros_like(sh_scr)
            hc_scr[...] = jnp.zeros_like(hc_scr)
        prevs = []
        for j in range(ntap):
            sh_scr[j, ns:ns + g_, :] = us[ns - ntap + j]
            prevs.append(sh_scr[j, ns - 1:ns - 1 + g_, :])
            sh_scr[j, ns - 1:ns, :] = sh_scr[j, ns - 1 + g_:ns + g_, :]
    else:
        prevs = [cbuf_ref[j] for j in range(ntap)]

    def tap(k, s):
        return us[k - s] if k - s >= 0 else prevs[k - s + ntap]

    cw = cw_ref[...]
    cbias = cb_ref[...]
    xcs = []
    for k in range(ns):
        acc = cbias + us[k] * cw[CONV_W - 1:CONV_W, :]
        for s in range(1, CONV_W):
            acc = acc + tap(k, s) * cw[CONV_W - 1 - s:CONV_W - s, :]
        xcs.append(acc)
    xc = jnp.concatenate(xcs, axis=0)
    xcb = xc.astype(BF16)
    rs, is_ = [], []
    for n in range(RNN_BLOCKS):
        ga = _dot(xcb[:, n * RNN_BW:(n + 1) * RNN_BW], wax_ref[n])
        rs.append(ga[:, :RNN_BW])
        is_.append(ga[:, RNN_BW:])
    r = jax.nn.sigmoid(jnp.concatenate(rs, axis=-1) + ba_ref[...])
    i_ = jax.nn.sigmoid(jnp.concatenate(is_, axis=-1) + bx_ref[...])
    lam = lam_ref[...]
    sp_neg_lam = jnp.maximum(-lam, 0.0) + jnp.log1p(jnp.exp(-jnp.abs(lam)))
    log_a = (-LRU_C) * r * sp_neg_lam
    a = jnp.exp(log_a)
    bb = jnp.sqrt(1.0 - jnp.exp(2.0 * log_a)) * (i_ * xc)

    aa = [a[0:g_]]
    bs = [bb[0:g_]]
    for k in range(1, ns):
        ak = a[k * g_:(k + 1) * g_]
        aa.append(ak * aa[-1])
        bs.append(ak * bs[-1] + bb[k * g_:(k + 1) * g_])
    if is_prompt:
        c = hc_scr[...]
        for g in range(g_):
            cin_scr[g:g + 1, :] = c
            c = aa[-1][g:g + 1, :] * c + bs[-1][g:g + 1, :]
        hc_scr[...] = c
        cin = cin_scr[...]
    else:
        cin = h0_ref[...]
    hs = [aa[k] * cin + bs[k] for k in range(ns)]
    hp = jnp.concatenate(hs, axis=0)
    y = xp + _dot((hp * _silu(gate)).astype(BF16), wout_ref[...])
    for k in range(ns):
        if is_prompt:
            y_ref[0, pl.ds(k, g_, stride=ns), :] = y[k * g_:(k + 1) * g_]
        else:
            y_ref[pl.ds(k, g_, stride=ns), :] = y[k * g_:(k + 1) * g_]

    if is_prompt:
        @pl.when(tb == pl.num_programs(1) - 1)
        def _():
            hl_ref[0] = hs[-1][g_ - 1:g_, :]
            for j in range(ntap):
                cv_ref[0, j:j + 1, :] = us[ns - ntap + j][g_ - 1:g_, :]
    else:
        hl_ref[...] = hs[-1]
        for j in range(ntap):
            cv_ref[j] = us[ns - ntap + j]


def _odd_prompt_unused(x, w, tt):
    b, t, d = x.shape
    g_ = tt // SUBLANES
    wl = [w['ln'], w['wu'], w['wg'], w['cw'], w['cb'], w['wax'], w['ba'], w['bx'], w['lam'], w['wout']]
    return pl.pallas_call(
        functools.partial(_odd_kernel, n_grp=g_, is_prompt=True),
        grid=(b, t // tt),
        in_specs=[pl.BlockSpec((1, tt, d), lambda i, j: (i, j, 0))] + [_full(a.shape) for a in wl],
        out_specs=[pl.BlockSpec((1, tt, d), lambda i, j: (i, j, 0)),
                   pl.BlockSpec((1, 1, D_RNN), lambda i, j: (i, 0, 0)),
                   pl.BlockSpec((1, CONV_W - 1, D_RNN), lambda i, j: (i, 0, 0))],
        out_shape=[jax.ShapeDtypeStruct((b, t, d), F32), jax.ShapeDtypeStruct((b, 1, D_RNN), F32),
                   jax.ShapeDtypeStruct((b, CONV_W - 1, D_RNN), F32)],
        scratch_shapes=[pltpu.VMEM((CONV_W - 1, g_ + SUBLANES, D_RNN), F32),
                        pltpu.VMEM((g_, D_RNN), F32), pltpu.VMEM((1, D_RNN), F32)],
        compiler_params=_params(("parallel", "arbitrary")),
        name="odd_prompt",
    )(x, *wl)


def _odd_sample_unused(x2, h0, cbuf, w, n_seq_blk):
    n, d = x2.shape
    b = h0.shape[0]
    rows = n_seq_blk * SUBLANES
    wl = [w['ln'], w['wu'], w['wg'], w['cw'], w['cb'], w['wax'], w['ba'], w['bx'], w['lam'], w['wout']]
    return pl.pallas_call(
        functools.partial(_odd_kernel, n_grp=n_seq_blk, is_prompt=False),
        grid=(b // n_seq_blk,),
        in_specs=[pl.BlockSpec((rows, d), lambda i: (i, 0)),
                  pl.BlockSpec((n_seq_blk, D_RNN), lambda i: (i, 0)),
                  pl.BlockSpec((CONV_W - 1, n_seq_blk, D_RNN), lambda i: (0, i, 0))]
                 + [_full(a.shape) for a in wl],
        out_specs=[pl.BlockSpec((rows, d), lambda i: (i, 0)),
                   pl.BlockSpec((n_seq_blk, D_RNN), lambda i: (i, 0)),
                   pl.BlockSpec((CONV_W - 1, n_seq_blk, D_RNN), lambda i: (0, i, 0))],
        out_shape=[jax.ShapeDtypeStruct((n, d), F32), jax.ShapeDtypeStruct((b, D_RNN), F32),
                   jax.ShapeDtypeStruct((CONV_W - 1, b, D_RNN), F32)],
        compiler_params=_params(("parallel",)),
        name="odd_sample",
    )(x2, h0, cbuf, *wl)


def _pad_heads(v, n_used):
    lead = v.shape[:-1]
    v = v.reshape(lead + (MLA_HEADS, n_used))
    v = jnp.pad(v, [(0, 0)] * len(lead) + [(0, 0), (0, HEAD_PAD - n_used)])
    return v.reshape(lead + (MLA_PADW,))


def _pack_even(j, ln_even, w_in_even, gla_w_f2, gla_b_f, gla_out_norm, mla_q_norm, mla_kv_norm, mla_w_uq,
               mla_w_ukv, mla_qh_norm, mla_kh_norm, w_out_even):
    wi = w_in_even[j]
    d = wi.shape[0]
    o = 0
    seg = {}
    for name, width in (('qa', GLA_QK), ('ka', GLA_QK), ('va', GLA_VW), ('ga', GLA_VW), ('fa', GLA_RANK),
                        ('cq', Q_LORA), ('ckv', KV_LORA), ('kr', MLA_ROPE), ('gb', MLA_VW)):
        seg[name] = wi[:, o:o + width]
        o += width
    z = lambda n: jnp.zeros((d, n), wi.dtype)
    misc = jnp.concatenate([seg['fa'], z(ROPE_LANE0 - GLA_RANK), seg['kr'], z(MISC_W - ROPE_LANE0 - MLA_ROPE)], 1)
    win = jnp.concatenate([seg['qa'], seg['ka'], seg['va'], seg['ga'], seg['cq'], seg['ckv'], seg['gb'], misc], 1)
    wf2 = jnp.pad(gla_w_f2[j], ((0, MISC_W - GLA_RANK), (0, 0)))
    ukv = mla_w_ukv[j].reshape(KV_LORA, MLA_HEADS, MLA_NOPE + MLA_V)
    wuk = ukv[:, :, :MLA_NOPE].reshape(KV_LORA, MLA_HEADS * MLA_NOPE)
    wuv = ukv[:, :, MLA_NOPE:].reshape(KV_LORA, MLA_VW)
    khp = jnp.pad(mla_kh_norm[j], (0, HEAD_PAD - MLA_QK))
    return {
        'ln': ln_even[j][None, :],
        'win': win.astype(BF16),
        'wf2': wf2.astype(BF16),
        'bf': gla_b_f[j][None, :],
        'qn': mla_q_norm[j][None, :],
        'kvn': mla_kv_norm[j][None, :],
        'wuq': _pad_heads(mla_w_uq[j], MLA_QK).astype(BF16),
        'wukp': _pad_heads(wuk, MLA_NOPE).astype(BF16),
        'wuk': wuk.astype(BF16),
        'wuv': wuv.astype(BF16),
        'qh': jnp.pad(mla_qh_norm[j], (0, HEAD_PAD - MLA_QK))[None, :],
        'kh': khp[None, :],
        'khg': jnp.tile(khp, MLA_HEADS)[None, :],
        'gr': mla_kh_norm[j][MLA_NOPE:][None, :],
        'on': gla_out_norm[j][None, :],
        'wout_a': w_out_even[j][:GLA_VW].astype(BF16),
        'wout_b': w_out_even[j][GLA_VW:].astype(BF16),
    }


def _pack_odd(j, ln_odd, w_in_odd, conv_w, conv_b, rg_w_a, rg_b_a, rg_w_x, rg_b_x, rg_lambda, w_out_odd):
    return {
        'ln': ln_odd[j][None, :],
        'wu': w_in_odd[j][:, :D_RNN].astype(BF16),
        'wg': w_in_odd[j][:, D_RNN:].astype(BF16),
        'cw': conv_w[j],
        'cb': conv_b[j][None, :],
        'wax': jnp.concatenate([rg_w_a[j], rg_w_x[j]], axis=-1).astype(BF16),
        'ba': rg_b_a[j][None, :],
        'bx': rg_b_x[j][None, :],
        'lam': rg_lambda[j][None, :],
        'wout': w_out_odd[j].astype(BF16),
    }


def _rope_angles(pos):
    inv_freq = ROPE_THETA ** (-jnp.arange(HALF_ROPE, dtype=F32) / HALF_ROPE)
    ang = pos.astype(F32)[:, None] * inv_freq[None, :]
    return jnp.cos(ang), jnp.sin(ang)


def _rope_tile_tables(pos):
    cos, sin = _rope_angles(pos)
    n = pos.shape[0]
    z = lambda w_: jnp.zeros((n, w_), F32)
    tail = LANES - ROPE_LANE0 - MLA_ROPE
    c = jnp.concatenate([jnp.ones((n, ROPE_LANE0), F32), cos, cos, jnp.ones((n, tail), F32)], 1)
    sp = jnp.concatenate([z(ROPE_LANE0 + HALF_ROPE), sin, z(tail)], 1)
    sm = jnp.concatenate([z(ROPE_LANE0), -sin, z(HALF_ROPE + tail)], 1)
    return c, sp, sm


def _rope_pair_tables(pos):
    cos, sin = _rope_angles(pos)
    return jnp.concatenate([cos, cos], 1), jnp.concatenate([sin, sin], 1)


def _pick(n, prefs):
    for p in prefs:
        if n % p == 0:
            return p
    return n


def kernel(x_prompt, x_sample, state_gla, cache_ckv, cache_kr, state_rglru_h, state_rglru_conv, page_table,
           ln_even, w_in_even, gla_w_f2, gla_b_f, gla_out_norm, mla_q_norm, mla_kv_norm, mla_w_uq, mla_w_ukv,
           mla_qh_norm, mla_kh_norm, w_out_even, ln_odd, w_in_odd, conv_w, conv_b, rg_w_a, rg_b_a, rg_w_x,
           rg_b_x, rg_lambda, w_out_odd):
    bp, tp, d = x_prompt.shape
    bs, ts, _ = x_sample.shape
    n_pages = page_table.shape[1]
    past_len = n_pages * PAGE_SIZE
    depth = ln_even.shape[0] + ln_odd.shape[0]
    assert ts == SUBLANES, "sample group is handled as one 8-row slot group per sequence"

    tabs_p = _rope_tile_tables(jnp.arange(tp))
    pos_s = past_len + jnp.arange(ts)
    tabs_s = _rope_tile_tables(pos_s)
    pair_new = _rope_pair_tables(pos_s)
    pair_past = _rope_pair_tables(jnp.arange(past_len))
    pt_flat = page_table.reshape(-1).astype(jnp.int32)

    tm_p = _pick(tp, (256, 128, 64, 32, 16, 8))
    tm_s = _pick(bs * ts, (256, 128, 64, 32, 16, 8))
    tabs_s = tuple(jnp.tile(a, (tm_s // ts, 1)) for a in tabs_s)
    chunk_p = min(GLA_CHUNK, tp)
    tblk_p = _pick(tp, (256, 128, 64)) if tp >= GLA_CHUNK else tp
    tq = _pick(tp, (512, 256, 128))
    tt_odd = _pick(tp, (256, 128, 64))
    n_pg = _pick(n_pages, (8, 4, 2, 1))
    nsb = _pick(bs, (32, 16, 8))

    yp = x_prompt.reshape(bp * tp, d)
    ys = x_sample.reshape(bs * ts, d)
    outs = {k: [] for k in ('gla_p', 'ckv_p', 'kr_p', 'rh_p', 'rc_p', 'gla_s', 'ckv_s', 'kr_s', 'rh_s', 'rc_s')}
    for layer in range(depth):
        j = layer // 2
        if layer % 2 == 0:
            w = _pack_even(j, ln_even, w_in_even, gla_w_f2, gla_b_f, gla_out_norm, mla_q_norm, mla_kv_norm,
                           mla_w_uq, mla_w_ukv, mla_qh_norm, mla_kh_norm, w_out_even)
            qa, ka, va, lf, ga, gb, q, k, v, ckvn, misc = _even_front(yp, w, tabs_p, tm_p, tp)
            r3 = lambda a: a.reshape(bp, tp, a.shape[-1])
            oa, st = _gla_scan(r3(qa), r3(ka), r3(va), r3(lf), None, chunk_p, tblk_p)
            ob = _flash_attn(r3(q), r3(k), r3(v), tq)
            yp = _even_back(yp, oa.reshape(bp * tp, GLA_VW), ga, ob.reshape(bp * tp, MLA_VW), gb, w, tm_p)
            outs['gla_p'].append(jnp.swapaxes(st, -1, -2))
            outs['ckv_p'].append(ckvn.reshape(bp, tp, KV_LORA))
            outs['kr_p'].append(misc[:, ROPE_LANE0:ROPE_LANE0 + MLA_ROPE].reshape(bp, tp, MLA_ROPE))
            qa, ka, va, lf, ga, gb, q, k, v, ckvn, misc = _even_front(ys, w, tabs_s, tm_s, ts)
            r3 = lambda a: a.reshape(bs, ts, a.shape[-1])
            s0t = jnp.swapaxes(state_gla[j], -1, -2)
            oa, st = _gla_scan(r3(qa), r3(ka), r3(va), r3(lf), s0t, ts, ts)
            krs = misc[:, ROPE_LANE0:ROPE_LANE0 + MLA_ROPE].reshape(bs, ts, MLA_ROPE)
            q4 = q.reshape(bs, ts, MLA_HEADS, HEAD_PAD)
            qrope = jnp.swapaxes(q4[..., ROPE_LANE0:ROPE_LANE0 + MLA_ROPE], 1, 2)
            qrope = qrope.reshape(bs, MLA_HEADS * ts, MLA_ROPE)
            qrot = jnp.concatenate([qrope[..., HALF_ROPE:], -qrope[..., :HALF_ROPE]], axis=-1)
            ob = _paged_attn(j, pt_flat, n_pages, r3(q), qrope, qrot, r3(ckvn), krs, pair_new, pair_past, w,
                             cache_ckv, cache_kr, n_pg)
            ys = _even_back(ys, oa.reshape(bs * ts, GLA_VW), ga, ob.reshape(bs * ts, MLA_VW), gb, w, tm_s)
            outs['gla_s'].append(jnp.swapaxes(st, -1, -2))
            outs['ckv_s'].append(ckvn.reshape(bs, ts, KV_LORA))
            outs['kr_s'].append(krs)
        else:
            w = _pack_odd(j, ln_odd, w_in_odd, conv_w, conv_b, rg_w_a, rg_b_a, rg_w_x, rg_b_x, rg_lambda,
                          w_out_odd)
            y3, hl, cv = _odd_prompt(yp.reshape(bp, tp, d), w, tt_odd)
            yp = y3.reshape(bp * tp, d)
            outs['rh_p'].append(hl.reshape(bp, D_RNN))
            outs['rc_p'].append(cv)
            ext = jnp.concatenate([jnp.zeros((bs, ts - CONV_W, D_RNN), F32), state_rglru_h[j][:, None, :],
                                   state_rglru_conv[j]], axis=1).reshape(bs * ts, D_RNN)
            ys, hfull, ufull = _odd_sample(ys, ext, w, tm_s)
            outs['rh_s'].append(hfull.reshape(bs, ts, D_RNN)[:, ts - 1])
            outs['rc_s'].append(ufull.reshape(bs, ts, D_RNN)[:, ts - (CONV_W - 1):])
    st_ = lambda name: jnp.stack(outs[name])
    return (yp.reshape(bp, tp, d), ys.reshape(bs, ts, d),
            st_('gla_p'), st_('ckv_p'), st_('kr_p'), st_('rh_p'), st_('rc_p'),
            st_('gla_s'), st_('ckv_s'), st_('kr_s'), st_('rh_s'), st_('rc_s'))
```

```python
import functools

import jax
import jax.numpy as jnp
from jax import lax
from jax.experimental import pallas as pl
from jax.experimental.pallas import tpu as pltpu

F32 = jnp.float32
BF16 = jnp.bfloat16
EPS = 1e-6

GLA_HEADS = 4
GLA_DK = 64
GLA_DV = 128
GLA_RANK = 16
GLA_TAU = 16.0
GLA_CHUNK = 64
GLA_QK = GLA_HEADS * GLA_DK
GLA_VW = GLA_HEADS * GLA_DV
MLA_HEADS = 8
MLA_NOPE = 64
MLA_ROPE = 32
MLA_V = 64
MLA_QK = MLA_NOPE + MLA_ROPE
MLA_VW = MLA_HEADS * MLA_V
MLA_NOPEW = MLA_HEADS * MLA_NOPE
Q_LORA = 384
KV_LORA = 256
ROPE_THETA = 10000.0
PAGE_SIZE = 128
D_RNN = 1280
RNN_BLOCKS = 10
RNN_BW = D_RNN // RNN_BLOCKS
CONV_W = 4
LRU_C = 8.0

LANES = 128
SUBLANES = 8
HEAD_PAD = LANES
MLA_PADW = MLA_HEADS * HEAD_PAD
HALF_ROPE = MLA_ROPE // 2
ROPE_LANE0 = MLA_NOPE
MISC_W = LANES
EXP_CLAMP = 80.0
NEG_BIG = -1e30
VMEM_LIMIT = 56 * 1024 * 1024
PAGED_KEYS_PER_STEP = 2048

NT_DIMS = (((1,), (1,)), ((), ()))
TN_DIMS = (((0,), (0,)), ((), ()))


def _dot(a, b):
    return jnp.dot(a, b, preferred_element_type=F32)


def _dot_nt(a, b):
    return lax.dot_general(a, b, NT_DIMS, preferred_element_type=F32)


def _dot_tn(a, b):
    return lax.dot_general(a, b, TN_DIMS, preferred_element_type=F32)


def _rms(x, g, n=None):
    n = x.shape[-1] if n is None else n
    ss = jnp.sum(x * x, axis=-1, keepdims=True) * (1.0 / n)
    return x * lax.rsqrt(ss + EPS) * g


def _sigmoid(x):
    return 0.5 * jnp.tanh(0.5 * x) + 0.5


def _silu(x):
    return x * _sigmoid(x)


def _log_sigmoid(x):
    return jnp.minimum(x, 0.0) - jnp.log1p(jnp.exp(-jnp.abs(x)))


def _rope_tile(t, c, sp, sm):
    return t * c + pltpu.roll(t, HALF_ROPE, 1) * sp + pltpu.roll(t, LANES - HALF_ROPE, 1) * sm


def _params(sem):
    return pltpu.CompilerParams(dimension_semantics=sem, vmem_limit_bytes=VMEM_LIMIT)


def _full(shape):
    nd = len(shape)
    return pl.BlockSpec(shape, lambda *_: (0,) * nd)


def _resident(shape):
    nd = len(shape)
    return pl.BlockSpec(shape, lambda *_: (0,) * nd, pipeline_mode=pl.Buffered(1))


def _even_front_kernel(x_ref, ln_ref, win_ref, wf2_ref, bf_ref, qn_ref, kvn_ref, wuq_ref, wuk_ref,
                       wuv_ref, qh_ref, kh_ref, c_ref, sp_ref, sm_ref,
                       qa_ref, ka_ref, va_ref, lf_ref, ga_ref, gb_ref, q_ref, k_ref, v_ref,
                       ckv_ref, misc_ref):
    x = x_ref[...]
    xn = _rms(x, ln_ref[...])
    z = _dot(xn.astype(BF16), win_ref[...])
    o = 0
    qa_ref[...] = z[:, o:o + GLA_QK] * (GLA_DK ** -0.5); o += GLA_QK
    ka_ref[...] = z[:, o:o + GLA_QK]; o += GLA_QK
    va_ref[...] = z[:, o:o + GLA_VW]; o += GLA_VW
    ga_ref[...] = _silu(z[:, o:o + GLA_VW]); o += GLA_VW
    cq = z[:, o:o + Q_LORA]; o += Q_LORA
    ckv = z[:, o:o + KV_LORA]; o += KV_LORA
    gb_ref[...] = _silu(z[:, o:o + MLA_VW]); o += MLA_VW
    misc = z[:, o:o + MISC_W]
    misc_ref[...] = misc

    f = _dot(misc.astype(BF16), wf2_ref[...]) + bf_ref[...]
    lf_ref[...] = _log_sigmoid(f) * (1.0 / GLA_TAU)

    c = c_ref[...]
    sp = sp_ref[...]
    sm = sm_ref[...]

    cqn = _rms(cq, qn_ref[...])
    qraw = _dot(cqn.astype(BF16), wuq_ref[...])
    qh = qh_ref[...]
    for h in range(MLA_HEADS):
        t = qraw[:, h * HEAD_PAD:(h + 1) * HEAD_PAD]
        t = _rope_tile(_rms(t, qh, MLA_QK), c, sp, sm) * (MLA_QK ** -0.5)
        q_ref[:, h * HEAD_PAD:(h + 1) * HEAD_PAD] = t.astype(BF16)

    ckvn = _rms(ckv, kvn_ref[...])
    ckv_ref[...] = ckvn
    cb = ckvn.astype(BF16)
    kraw = _dot(cb, wuk_ref[...])
    lane = lax.broadcasted_iota(jnp.int32, misc.shape, 1)
    krt = jnp.where(lane >= ROPE_LANE0, misc, 0.0)
    kh = kh_ref[...]
    for h in range(MLA_HEADS):
        t = kraw[:, h * HEAD_PAD:(h + 1) * HEAD_PAD] + krt
        t = _rope_tile(_rms(t, kh, MLA_QK), c, sp, sm)
        k_ref[:, h * HEAD_PAD:(h + 1) * HEAD_PAD] = t.astype(BF16)
    v_ref[...] = _dot(cb, wuv_ref[...]).astype(BF16)


def _even_front(x2, w, tabs, tm, rows_per_seq):
    n, d = x2.shape
    nblk_per_seq = rows_per_seq // tm if rows_per_seq >= tm else None
    row = lambda w_: pl.BlockSpec((tm, w_), lambda i: (i, 0))
    if nblk_per_seq is not None:
        tab = pl.BlockSpec((tm, LANES), lambda i: (i % nblk_per_seq, 0))
    else:
        tab = pl.BlockSpec((tm, LANES), lambda i: (0, 0))
    ins = [x2, w['ln'], w['win'], w['wf2'], w['bf'], w['qn'], w['kvn'], w['wuq'], w['wukp'],
           w['wuv'], w['qh'], w['kh'], tabs[0], tabs[1], tabs[2]]
    in_specs = [row(d)] + [_resident(a.shape) for a in ins[1:12]] + [tab, tab, tab]
    outs = [(GLA_QK, F32), (GLA_QK, F32), (GLA_VW, F32), (GLA_QK, F32), (GLA_VW, F32), (MLA_VW, F32),
            (MLA_PADW, BF16), (MLA_PADW, BF16), (MLA_VW, BF16), (KV_LORA, F32), (MISC_W, F32)]
    return pl.pallas_call(
        _even_front_kernel,
        grid=(n // tm,),
        in_specs=in_specs,
        out_specs=[row(w_) for w_, _ in outs],
        out_shape=[jax.ShapeDtypeStruct((n, w_), dt) for w_, dt in outs],
        compiler_params=_params(("parallel",)),
        name="even_front",
    )(*ins)


def _gla_kernel(*refs, chunk, n_chunks, has_s0):
    if has_s0:
        q_ref, k_ref, v_ref, g_ref, s0_ref, o_ref, st_ref, s_scr = refs
    else:
        q_ref, k_ref, v_ref, g_ref, o_ref, st_ref, s_scr = refs
    tb = pl.program_id(1)

    @pl.when(tb == 0)
    def _():
        if has_s0:
            s_scr[...] = s0_ref[0]
        else:
            s_scr[...] = jnp.zeros_like(s_scr)

    c_ = chunk
    row = lax.broadcasted_iota(jnp.int32, (c_, c_), 0)
    col = lax.broadcasted_iota(jnp.int32, (c_, c_), 1)
    tri = row >= col
    tri_bf = jnp.where(tri, 1.0, 0.0).astype(BF16)
    mid_row = c_ // 2 - 1
    for c in range(n_chunks):
        sl = slice(c * c_, (c + 1) * c_)
        g = g_ref[0, sl, :]
        g_hi = g.astype(BF16)
        g_lo = (g - g_hi.astype(F32)).astype(BF16)
        cum = _dot(tri_bf, g_hi) + _dot(tri_bf, g_lo)
        tot = cum[c_ - 1:c_, :]
        mid = cum[mid_row:mid_row + 1, :]
        q = q_ref[0, sl, :]
        k = k_ref[0, sl, :]
        qd = (q * jnp.exp(cum)).astype(BF16)
        qm = (q * jnp.exp(jnp.minimum(cum - mid, EXP_CLAMP))).astype(BF16)
        km = (k * jnp.exp(jnp.minimum(mid - cum, EXP_CLAMP))).astype(BF16)
        kd = (k * jnp.exp(tot - cum)).astype(BF16)
        etot = jnp.exp(tot)
        for h in range(GLA_HEADS):
            hs = slice(h * GLA_DK, (h + 1) * GLA_DK)
            vs = slice(h * GLA_DV, (h + 1) * GLA_DV)
            a = _dot_nt(qm[:, hs], km[:, hs])
            a = jnp.where(tri, a, 0.0).astype(BF16)
            vh = v_ref[0, sl, vs].astype(BF16)
            st = s_scr[h]
            o_ref[0, sl, vs] = _dot_nt(qd[:, hs], st.astype(BF16)) + _dot(a, vh)
            s_scr[h] = st * etot[:, hs] + _dot_tn(vh, kd[:, hs])

    @pl.when(tb == pl.num_programs(1) - 1)
    def _():
        st_ref[0] = s_scr[...]


def _gla_scan(qa, ka, va, lf, s0t, chunk, tblk):
    b, t, _ = qa.shape
    n_chunks = tblk // chunk
    has_s0 = s0t is not None
    seq = lambda w_: pl.BlockSpec((1, tblk, w_), lambda i, j: (i, j, 0))
    st_spec = pl.BlockSpec((1, GLA_HEADS, GLA_DV, GLA_DK), lambda i, j: (i, 0, 0, 0))
    ins = [qa, ka, va, lf] + ([s0t] if has_s0 else [])
    in_specs = [seq(GLA_QK), seq(GLA_QK), seq(GLA_VW), seq(GLA_QK)] + ([st_spec] if has_s0 else [])
    return pl.pallas_call(
        functools.partial(_gla_kernel, chunk=chunk, n_chunks=n_chunks, has_s0=has_s0),
        grid=(b, t // tblk),
        in_specs=in_specs,
        out_specs=[seq(GLA_VW), st_spec],
        out_shape=[jax.ShapeDtypeStruct((b, t, GLA_VW), F32),
                   jax.ShapeDtypeStruct((b, GLA_HEADS, GLA_DV, GLA_DK), F32)],
        scratch_shapes=[pltpu.VMEM((GLA_HEADS, GLA_DV, GLA_DK), F32)],
        compiler_params=_params(("parallel", "arbitrary")),
        name="gla_scan",
    )(*ins)


def _flash_kernel(q_ref, k_ref, v_ref, o_ref, m_scr, l_scr, acc_scr, *, tq, tk):
    qi = pl.program_id(2)
    ki = pl.program_id(3)

    @pl.when(ki == 0)
    def _():
        m_scr[...] = jnp.full_like(m_scr, NEG_BIG)
        l_scr[...] = jnp.zeros_like(l_scr)
        acc_scr[...] = jnp.zeros_like(acc_scr)

    def update(diagonal):
        v = v_ref[0]
        lane = lax.broadcasted_iota(jnp.int32, (tq, LANES), 1)
        acc = acc_scr[...]
        for hh in range(2):
            qh = q_ref[0, :, hh * HEAD_PAD:(hh + 1) * HEAD_PAD]
            kh = k_ref[0, :, hh * HEAD_PAD:(hh + 1) * HEAD_PAD]
            s = _dot_nt(qh, kh)
            if diagonal:
                row = lax.broadcasted_iota(jnp.int32, (tq, tk), 0)
                col = lax.broadcasted_iota(jnp.int32, (tq, tk), 1)
                s = jnp.where(col <= row, s, NEG_BIG)
            m_prev = m_scr[hh]
            m_new = jnp.maximum(m_prev, jnp.max(s, axis=-1, keepdims=True))
            alpha = jnp.exp(m_prev - m_new)
            p = jnp.exp(s - m_new[:, 0:1])
            l_scr[hh] = alpha * l_scr[hh] + jnp.sum(p, axis=-1, keepdims=True)
            m_scr[hh] = m_new
            pv = _dot(p.astype(BF16), v)
            mine = (lane >= hh * MLA_V) & (lane < (hh + 1) * MLA_V)
            acc = jnp.where(mine, alpha * acc + pv, acc)
        acc_scr[...] = acc

    @pl.when(ki < qi)
    def _():
        update(False)

    @pl.when(ki == qi)
    def _():
        update(True)

    @pl.when(ki == pl.num_programs(3) - 1)
    def _():
        lane = lax.broadcasted_iota(jnp.int32, (tq, LANES), 1)
        l = jnp.where(lane < MLA_V, l_scr[0], l_scr[1])
        o_ref[0] = acc_scr[...] / l


def _flash_attn(q, k, v, tq):
    b, t, _ = q.shape
    tk = tq
    nq = t // tq
    hp = MLA_HEADS // 2
    return pl.pallas_call(
        functools.partial(_flash_kernel, tq=tq, tk=tk),
        grid=(b, hp, nq, nq),
        in_specs=[pl.BlockSpec((1, tq, 2 * HEAD_PAD), lambda bi, h, i, j: (bi, i, h)),
                  pl.BlockSpec((1, tk, 2 * HEAD_PAD), lambda bi, h, i, j: (bi, jnp.minimum(i, j), h)),
                  pl.BlockSpec((1, tk, 2 * MLA_V), lambda bi, h, i, j: (bi, jnp.minimum(i, j), h))],
        out_specs=pl.BlockSpec((1, tq, 2 * MLA_V), lambda bi, h, i, j: (bi, i, h)),
        out_shape=jax.ShapeDtypeStruct((b, t, MLA_VW), F32),
        scratch_shapes=[pltpu.VMEM((2, tq, LANES), F32), pltpu.VMEM((2, tq, LANES), F32),
                        pltpu.VMEM((tq, LANES), F32)],
        compiler_params=_params(("parallel", "parallel", "parallel", "arbitrary")),
        name="flash_attn",
    )(q, k, v)


def _paged_kernel(pt_ref, q_ref, qrr_ref, latn_ref, krnt_ref, cosn_ref, sinn_ref, cos_ref, sin_ref,
                  khg_ref, grc_ref, wukp_ref, wukt_ref, wuv_ref, ckv_hbm, krt_hbm,
                  o_ref, lat_buf, krt_buf, sem, lhs_scr, m_scr, l_scr, acc_scr, *, layer, n_pg, t_new):
    p_idx = pl.program_id(1)
    steps = pl.num_programs(1)
    step = pl.program_id(0) * steps + p_idx
    total = pl.num_programs(0) * steps
    slot = lax.rem(step, 2)
    nq = MLA_HEADS * t_new
    tt = n_pg * PAGE_SIZE

    def page_copies(st, sl):
        cps = []
        for g in range(n_pg):
            page = pt_ref[st * n_pg + g]
            rows = pl.ds(g * PAGE_SIZE, PAGE_SIZE)
            cps.append(pltpu.make_async_copy(ckv_hbm.at[layer, page], lat_buf.at[sl, rows, :], sem.at[sl, 0]))
            cps.append(pltpu.make_async_copy(krt_hbm.at[layer, page], krt_buf.at[sl, :, rows], sem.at[sl, 1]))
        return cps

    @pl.when(step == 0)
    def _():
        for cp in page_copies(0, 0):
            cp.start()

    @pl.when(step + 1 < total)
    def _():
        for cp in page_copies(step + 1, 1 - slot):
            cp.start()

    for cp in page_copies(step, slot):
        cp.wait()

    @pl.when(p_idx == 0)
    def _():
        qt = q_ref[0].astype(F32)
        qrep = jnp.concatenate([qt] * MLA_HEADS, axis=0)
        r_ = lax.broadcasted_iota(jnp.int32, qrep.shape, 0) // t_new
        c_ = lax.broadcasted_iota(jnp.int32, qrep.shape, 1) // HEAD_PAD
        qbd = jnp.where(r_ == c_, qrep * khg_ref[...], 0.0)
        lhs_scr[0:nq, :] = _dot_nt(qbd.astype(BF16), wukp_ref[...]).astype(BF16)
        lhs_scr[nq:, :] = wukt_ref[...]
        m_scr[...] = jnp.full_like(m_scr, NEG_BIG)
        l_scr[...] = jnp.zeros_like(l_scr)
        acc_scr[...] = jnp.zeros_like(acc_scr)

    grc = grc_ref[...]
    qrr = qrr_ref[0]

    def attend(cb, kt, cost, sint, mask):
        n = cb.shape[0]
        big = _dot_nt(lhs_scr[...], cb)
        k2 = jnp.sum(kt * kt, axis=0, keepdims=True)
        rows = []
        for h in range(MLA_HEADS):
            blk = big[nq + h * MLA_NOPE:nq + (h + 1) * MLA_NOPE]
            ssq = jnp.sum(blk * blk, axis=0, keepdims=True) + k2
            rows.append(jnp.broadcast_to(lax.rsqrt(ssq * (1.0 / MLA_QK) + EPS), (t_new, n)))
        rinv = jnp.concatenate(rows, axis=0)
        x = kt * grc
        xx = jnp.concatenate([x * cost, x * sint], axis=0).astype(BF16)
        s = (big[0:nq] + _dot(qrr, xx)) * rinv
        if mask is not None:
            s = jnp.where(mask, s, NEG_BIG)
        m_prev = m_scr[...]
        m_new = jnp.maximum(m_prev, jnp.max(s, axis=-1, keepdims=True))
        alpha = jnp.exp(m_prev - m_new)
        p = jnp.exp(s - m_new[:, 0:1])
        l_scr[...] = alpha * l_scr[...] + jnp.sum(p, axis=-1, keepdims=True)
        m_scr[...] = m_new
        acc_scr[...] = alpha[:, 0:1] * acc_scr[...] + _dot(p.astype(BF16), cb)

    ks = pl.ds(pl.multiple_of(p_idx * tt, tt), tt)
    attend(lat_buf[slot].astype(BF16), krt_buf[slot], cos_ref[:, ks], sin_ref[:, ks], None)

    @pl.when(p_idx == pl.num_programs(1) - 1)
    def _():
        qi = lax.broadcasted_iota(jnp.int32, (nq, t_new), 0) % t_new
        kj = lax.broadcasted_iota(jnp.int32, (nq, t_new), 1)
        attend(latn_ref[0].astype(BF16), krnt_ref[0], cosn_ref[...], sinn_ref[...], kj <= qi)
        out_lat = acc_scr[...] / l_scr[:, 0:1]
        er = lax.broadcasted_iota(jnp.int32, (nq, MLA_VW), 0) // t_new
        ec = lax.broadcasted_iota(jnp.int32, (nq, MLA_VW), 1) // MLA_V
        full = jnp.where(er == ec, _dot(out_lat.astype(BF16), wuv_ref[...]), 0.0)
        o = full[0:t_new]
        for h in range(1, MLA_HEADS):
            o = o + full[h * t_new:(h + 1) * t_new]
        o_ref[0] = o


def _paged_attn(layer, page_table_flat, n_pages, q3, qrr, latn, krnt, tabs_new, tabs_past, w,
                cache_ckv, cache_krt, n_pg):
    b, t_new, _ = q3.shape
    nq = MLA_HEADS * t_new
    steps = n_pages // n_pg
    tt = n_pg * PAGE_SIZE

    def per_b(shape):
        nd = len(shape)
        return pl.BlockSpec((1,) + shape[1:], lambda bi, p, pt: (bi,) + (0,) * (nd - 1))

    def const(shape):
        nd = len(shape)
        return pl.BlockSpec(shape, lambda bi, p, pt: (0,) * nd, pipeline_mode=pl.Buffered(1))

    hbm = pl.BlockSpec(memory_space=pl.ANY)
    ins = [q3, qrr, latn, krnt, tabs_new[0], tabs_new[1], tabs_past[0], tabs_past[1],
           w['khg'], w['grc'], w['wukp'], w['wukt'], w['wuv'], cache_ckv, cache_krt]
    in_specs = [per_b(a.shape) for a in ins[:4]] + [const(a.shape) for a in ins[4:13]] + [hbm, hbm]
    grid_spec = pltpu.PrefetchScalarGridSpec(
        num_scalar_prefetch=1,
        grid=(b, steps),
        in_specs=in_specs,
        out_specs=pl.BlockSpec((1, t_new, MLA_VW), lambda bi, p, pt: (bi, 0, 0)),
        scratch_shapes=[pltpu.VMEM((2, tt, KV_LORA), F32), pltpu.VMEM((2, MLA_ROPE, tt), F32),
                        pltpu.SemaphoreType.DMA((2, 2)),
                        pltpu.VMEM((nq + MLA_NOPEW, KV_LORA), BF16), pltpu.VMEM((nq, LANES), F32),
                        pltpu.VMEM((nq, LANES), F32), pltpu.VMEM((nq, KV_LORA), F32)],
    )
    return pl.pallas_call(
        functools.partial(_paged_kernel, layer=layer, n_pg=n_pg, t_new=t_new),
        grid_spec=grid_spec,
        out_shape=jax.ShapeDtypeStruct((b, t_new, MLA_VW), F32),
        compiler_params=_params(("arbitrary", "arbitrary")),
        name="paged_attn",
    )(page_table_flat, *ins)


def _even_back_kernel(x_ref, oa_ref, ga_ref, ob_ref, gb_ref, on_ref, wa_ref, wb_ref, y_ref):
    oa = oa_ref[...]
    on = on_ref[...]
    parts = []
    for h in range(GLA_HEADS):
        parts.append(_rms(oa[:, h * GLA_DV:(h + 1) * GLA_DV], on))
    a = jnp.concatenate(parts, axis=-1) * ga_ref[...]
    b_ = ob_ref[...] * gb_ref[...]
    y_ref[...] = x_ref[...] + _dot(a.astype(BF16), wa_ref[...]) + _dot(b_.astype(BF16), wb_ref[...])


def _even_back(x2, oa, ga, ob, gb, w, tm):
    n, d = x2.shape
    row = lambda w_: pl.BlockSpec((tm, w_), lambda i: (i, 0))
    ins = [x2, oa, ga, ob, gb, w['on'], w['wout_a'], w['wout_b']]
    return pl.pallas_call(
        _even_back_kernel,
        grid=(n // tm,),
        in_specs=[row(d), row(GLA_VW), row(GLA_VW), row(MLA_VW), row(MLA_VW)] + [_full(a.shape) for a in ins[5:]],
        out_specs=row(d),
        out_shape=jax.ShapeDtypeStruct((n, d), F32),
        compiler_params=_params(("parallel",)),
        name="even_back",
    )(*ins)


def _group_scan(a, b):
    rows, width = a.shape
    a = a.reshape(rows // SUBLANES, SUBLANES, width)
    b = b.reshape(rows // SUBLANES, SUBLANES, width)
    t = lax.broadcasted_iota(jnp.int32, a.shape, 1)
    for s in (1, 2, 4):
        keep = t >= s
        ar = pltpu.roll(a, s, 1)
        br = pltpu.roll(b, s, 1)
        b = jnp.where(keep, a * br + b, b)
        a = jnp.where(keep, a * ar, a)
    return a.reshape(rows, width), b.reshape(rows, width)


def _odd_gates(xc, wax_ref, ba_ref, bx_ref, lam_ref):
    xcb = xc.astype(BF16)
    rs, is_ = [], []
    for n in range(RNN_BLOCKS):
        ga = _dot(xcb[:, n * RNN_BW:(n + 1) * RNN_BW], wax_ref[n])
        rs.append(ga[:, :RNN_BW])
        is_.append(ga[:, RNN_BW:])
    r = _sigmoid(jnp.concatenate(rs, axis=-1) + ba_ref[...])
    i_ = _sigmoid(jnp.concatenate(is_, axis=-1) + bx_ref[...])
    lam = lam_ref[...]
    sp_neg_lam = jnp.maximum(-lam, 0.0) + jnp.log1p(jnp.exp(-jnp.abs(lam)))
    a = jnp.exp((-LRU_C) * r * sp_neg_lam)
    b = jnp.sqrt(1.0 - a * a) * (i_ * xc)
    return a, b


def _odd_prompt_kernel(x_ref, ln_ref, wu_ref, wg_ref, cw_ref, cb_ref, wax_ref, ba_ref, bx_ref, lam_ref,
                       wout_ref, y_ref, hl_ref, cv_ref, ush_scr, hc_scr, *, tt):
    tb = pl.program_id(1)
    ns = SUBLANES

    @pl.when(tb == 0)
    def _():
        ush_scr[0:ns, :] = jnp.zeros((ns, D_RNN), F32)
        hc_scr[...] = jnp.zeros_like(hc_scr)

    x = x_ref[0]
    xb = _rms(x, ln_ref[...]).astype(BF16)
    u = _dot(xb, wu_ref[...])
    gate = _dot(xb, wg_ref[...])
    ush_scr[ns:ns + tt, :] = u
    cw = cw_ref[...]
    xc = cb_ref[...] + u * cw[CONV_W - 1:CONV_W, :]
    for s in range(1, CONV_W):
        xc = xc + ush_scr[ns - s:ns - s + tt, :] * cw[CONV_W - 1 - s:CONV_W - s, :]
    ush_scr[0:ns, :] = ush_scr[tt:tt + ns, :]
    a, b = _odd_gates(xc, wax_ref, ba_ref, bx_ref, lam_ref)
    a, b = _group_scan(a, b)
    c = hc_scr[...]
    hs = []
    for g in range(tt // ns):
        hg = a[g * ns:(g + 1) * ns] * c + b[g * ns:(g + 1) * ns]
        hs.append(hg)
        c = hg[ns - 1:ns, :]
    hc_scr[...] = c
    h = jnp.concatenate(hs, axis=0)
    y_ref[0] = x + _dot((h * _silu(gate)).astype(BF16), wout_ref[...])

    @pl.when(tb == pl.num_programs(1) - 1)
    def _():
        hl_ref[0] = c
        cv_ref[0] = ush_scr[ns - (CONV_W - 1):ns, :]


def _odd_sample_kernel(x_ref, ext_ref, ln_ref, wu_ref, wg_ref, cw_ref, cb_ref, wax_ref, ba_ref, bx_ref, lam_ref,
                       wout_ref, y_ref, h_ref, u_ref):
    ns = SUBLANES
    x = x_ref[...]
    rows = x.shape[0]
    xb = _rms(x, ln_ref[...]).astype(BF16)
    u = _dot(xb, wu_ref[...])
    gate = _dot(xb, wg_ref[...])
    u_ref[...] = u
    ext = ext_ref[...]
    t = lax.broadcasted_iota(jnp.int32, (rows, D_RNN), 0) % ns
    cw = cw_ref[...]
    xc = cb_ref[...] + u * cw[CONV_W - 1:CONV_W, :]
    for s in range(1, CONV_W):
        prev = jnp.where(t < s, pltpu.roll(ext, rows + s - ns, 0), pltpu.roll(u, s, 0))
        xc = xc + prev * cw[CONV_W - 1 - s:CONV_W - s, :]
    a, b = _odd_gates(xc, wax_ref, ba_ref, bx_ref, lam_ref)
    h0_at0 = jnp.where(t == 0, pltpu.roll(ext, rows - (ns - CONV_W), 0), 0.0)
    _, h = _group_scan(a, b + a * h0_at0)
    h_ref[...] = h
    y_ref[...] = x + _dot((h * _silu(gate)).astype(BF16), wout_ref[...])


_ODD_WEIGHTS = ('ln', 'wu', 'wg', 'cw', 'cb', 'wax', 'ba', 'bx', 'lam', 'wout')


def _odd_prompt(x, w, tt):
    b, t, d = x.shape
    wl = [w[n] for n in _ODD_WEIGHTS]
    return pl.pallas_call(
        functools.partial(_odd_prompt_kernel, tt=tt),
        grid=(b, t // tt),
        in_specs=[pl.BlockSpec((1, tt, d), lambda i, j: (i, j, 0))] + [_resident(a.shape) for a in wl],
        out_specs=[pl.BlockSpec((1, tt, d), lambda i, j: (i, j, 0)),
                   pl.BlockSpec((1, 1, D_RNN), lambda i, j: (i, 0, 0)),
                   pl.BlockSpec((1, CONV_W - 1, D_RNN), lambda i, j: (i, 0, 0))],
        out_shape=[jax.ShapeDtypeStruct((b, t, d), F32), jax.ShapeDtypeStruct((b, 1, D_RNN), F32),
                   jax.ShapeDtypeStruct((b, CONV_W - 1, D_RNN), F32)],
        scratch_shapes=[pltpu.VMEM((tt + SUBLANES, D_RNN), F32), pltpu.VMEM((1, D_RNN), F32)],
        compiler_params=_params(("parallel", "arbitrary")),
        name="odd_prompt",
    )(x, *wl)


def _odd_sample(x2, ext, w, rows):
    n, d = x2.shape
    wl = [w[n_] for n_ in _ODD_WEIGHTS]
    row = lambda w_: pl.BlockSpec((rows, w_), lambda i: (i, 0))
    return pl.pallas_call(
        _odd_sample_kernel,
        grid=(n // rows,),
        in_specs=[row(d), row(D_RNN)] + [_resident(a.shape) for a in wl],
        out_specs=[row(d), row(D_RNN), row(D_RNN)],
        out_shape=[jax.ShapeDtypeStruct((n, d), F32), jax.ShapeDtypeStruct((n, D_RNN), F32),
                   jax.ShapeDtypeStruct((n, D_RNN), F32)],
        compiler_params=_params(("parallel",)),
        name="odd_sample",
    )(x2, ext, *wl)


def _pad_heads(v, n_used):
    lead = v.shape[:-1]
    v = v.reshape(lead + (MLA_HEADS, n_used))
    v = jnp.pad(v, [(0, 0)] * len(lead) + [(0, 0), (0, HEAD_PAD - n_used)])
    return v.reshape(lead + (MLA_PADW,))


def _pack_even(j, ln_even, w_in_even, gla_w_f2, gla_b_f, gla_out_norm, mla_q_norm, mla_kv_norm, mla_w_uq,
               mla_w_ukv, mla_qh_norm, mla_kh_norm, w_out_even):
    wi = w_in_even[j]
    d = wi.shape[0]
    o = 0
    seg = {}
    for name, width in (('qa', GLA_QK), ('ka', GLA_QK), ('va', GLA_VW), ('ga', GLA_VW), ('fa', GLA_RANK),
                        ('cq', Q_LORA), ('ckv', KV_LORA), ('kr', MLA_ROPE), ('gb', MLA_VW)):
        seg[name] = wi[:, o:o + width]
        o += width
    z = lambda n: jnp.zeros((d, n), wi.dtype)
    misc = jnp.concatenate([seg['fa'], z(ROPE_LANE0 - GLA_RANK), seg['kr'], z(MISC_W - ROPE_LANE0 - MLA_ROPE)], 1)
    win = jnp.concatenate([seg['qa'], seg['ka'], seg['va'], seg['ga'], seg['cq'], seg['ckv'], seg['gb'], misc], 1)
    wf2 = jnp.pad(gla_w_f2[j], ((0, MISC_W - GLA_RANK), (0, 0)))
    ukv = mla_w_ukv[j].reshape(KV_LORA, MLA_HEADS, MLA_NOPE + MLA_V)
    wuk = ukv[:, :, :MLA_NOPE].reshape(KV_LORA, MLA_NOPEW)
    wuv = ukv[:, :, MLA_NOPE:].reshape(KV_LORA, MLA_VW)
    khp = jnp.pad(mla_kh_norm[j], (0, HEAD_PAD - MLA_QK))
    return {
        'ln': ln_even[j][None, :],
        'win': win.astype(BF16),
        'wf2': wf2.astype(BF16),
        'bf': gla_b_f[j][None, :],
        'qn': mla_q_norm[j][None, :],
        'kvn': mla_kv_norm[j][None, :],
        'wuq': _pad_heads(mla_w_uq[j], MLA_QK).astype(BF16),
        'wukp': _pad_heads(wuk, MLA_NOPE).astype(BF16),
        'wukt': wuk.T.astype(BF16),
        'wuv': wuv.astype(BF16),
        'qh': jnp.pad(mla_qh_norm[j], (0, HEAD_PAD - MLA_QK))[None, :],
        'kh': khp[None, :],
        'khg': jnp.tile(khp, MLA_HEADS)[None, :],
        'grc': mla_kh_norm[j][MLA_NOPE:][:, None],
        'on': gla_out_norm[j][None, :],
        'wout_a': w_out_even[j][:GLA_VW].astype(BF16),
        'wout_b': w_out_even[j][GLA_VW:].astype(BF16),
    }


def _pack_odd(j, ln_odd, w_in_odd, conv_w, conv_b, rg_w_a, rg_b_a, rg_w_x, rg_b_x, rg_lambda, w_out_odd):
    return {
        'ln': ln_odd[j][None, :],
        'wu': w_in_odd[j][:, :D_RNN].astype(BF16),
        'wg': w_in_odd[j][:, D_RNN:].astype(BF16),
        'cw': conv_w[j],
        'cb': conv_b[j][None, :],
        'wax': jnp.concatenate([rg_w_a[j], rg_w_x[j]], axis=-1).astype(BF16),
        'ba': rg_b_a[j][None, :],
        'bx': rg_b_x[j][None, :],
        'lam': rg_lambda[j][None, :],
        'wout': w_out_odd[j].astype(BF16),
    }


def _rope_angles(pos):
    inv_freq = ROPE_THETA ** (-jnp.arange(HALF_ROPE, dtype=F32) / HALF_ROPE)
    ang = pos.astype(F32)[:, None] * inv_freq[None, :]
    return jnp.cos(ang), jnp.sin(ang)


def _rope_tile_tables(pos):
    cos, sin = _rope_angles(pos)
    n = pos.shape[0]
    z = lambda w_: jnp.zeros((n, w_), F32)
    tail = LANES - ROPE_LANE0 - MLA_ROPE
    c = jnp.concatenate([jnp.ones((n, ROPE_LANE0), F32), cos, cos, jnp.ones((n, tail), F32)], 1)
    sp = jnp.concatenate([z(ROPE_LANE0 + HALF_ROPE), sin, z(tail)], 1)
    sm = jnp.concatenate([z(ROPE_LANE0), -sin, z(HALF_ROPE + tail)], 1)
    return c, sp, sm


def _rope_pair_tables_t(pos):
    cos, sin = _rope_angles(pos)
    return jnp.concatenate([cos, cos], 1).T, jnp.concatenate([sin, sin], 1).T


def _pick(n, prefs):
    for p in prefs:
        if n % p == 0:
            return p
    return n


def kernel(x_prompt, x_sample, state_gla, cache_ckv, cache_kr, state_rglru_h, state_rglru_conv, page_table,
           ln_even, w_in_even, gla_w_f2, gla_b_f, gla_out_norm, mla_q_norm, mla_kv_norm, mla_w_uq, mla_w_ukv,
           mla_qh_norm, mla_kh_norm, w_out_even, ln_odd, w_in_odd, conv_w, conv_b, rg_w_a, rg_b_a, rg_w_x,
           rg_b_x, rg_lambda, w_out_odd):
    bp, tp, d = x_prompt.shape
    bs, ts, _ = x_sample.shape
    n_pages = page_table.shape[1]
    past_len = n_pages * PAGE_SIZE
    depth = ln_even.shape[0] + ln_odd.shape[0]
    assert ts == SUBLANES, "sample group is handled as one 8-row group per sequence"
    assert MLA_NOPE == MLA_V

    tm_p = _pick(tp, (512, 256, 128, 64, 32, 16, 8))
    tm_s = _pick(bs * ts, (512, 256, 128, 64, 32, 16, 8))
    chunk_p = min(GLA_CHUNK, tp)
    tblk_p = _pick(tp, (256, 128, 64)) if tp >= GLA_CHUNK else tp
    tq = _pick(tp, (512, 256, 128))
    tt_odd = _pick(tp, (256, 128, 64))
    n_pg = _pick(n_pages, (PAGED_KEYS_PER_STEP // PAGE_SIZE, 8, 4, 2, 1))

    tabs_p = _rope_tile_tables(jnp.arange(tp))
    pos_s = past_len + jnp.arange(ts)
    tabs_s = tuple(jnp.tile(a, (tm_s // ts, 1)) for a in _rope_tile_tables(pos_s))
    pair_new = _rope_pair_tables_t(pos_s)
    pair_past = _rope_pair_tables_t(jnp.arange(past_len))
    pt_flat = page_table.reshape(-1).astype(jnp.int32)
    cache_krt = jnp.swapaxes(cache_kr, -1, -2)

    yp = x_prompt.reshape(bp * tp, d)
    ys = x_sample.reshape(bs * ts, d)
    outs = {k: [] for k in ('gla_p', 'ckv_p', 'kr_p', 'rh_p', 'rc_p', 'gla_s', 'ckv_s', 'kr_s', 'rh_s', 'rc_s')}
    for layer in range(depth):
        j = layer // 2
        if layer % 2 == 0:
            w = _pack_even(j, ln_even, w_in_even, gla_w_f2, gla_b_f, gla_out_norm, mla_q_norm, mla_kv_norm,
                           mla_w_uq, mla_w_ukv, mla_qh_norm, mla_kh_norm, w_out_even)
            qa, ka, va, lf, ga, gb, q, k, v, ckvn, misc = _even_front(yp, w, tabs_p, tm_p, tp)
            r3 = lambda a: a.reshape(bp, tp, a.shape[-1])
            oa, st = _gla_scan(r3(qa), r3(ka), r3(va), r3(lf), None, chunk_p, tblk_p)
            ob = _flash_attn(r3(q), r3(k), r3(v), tq)
            yp = _even_back(yp, oa.reshape(bp * tp, GLA_VW), ga, ob.reshape(bp * tp, MLA_VW), gb, w, tm_p)
            outs['gla_p'].append(jnp.swapaxes(st, -1, -2))
            outs['ckv_p'].append(ckvn.reshape(bp, tp, KV_LORA))
            outs['kr_p'].append(misc[:, ROPE_LANE0:ROPE_LANE0 + MLA_ROPE].reshape(bp, tp, MLA_ROPE))
            qa, ka, va, lf, ga, gb, q, k, v, ckvn, misc = _even_front(ys, w, tabs_s, tm_s, ts)
            r3 = lambda a: a.reshape(bs, ts, a.shape[-1])
            s0t = jnp.swapaxes(state_gla[j], -1, -2)
            oa, st = _gla_scan(r3(qa), r3(ka), r3(va), r3(lf), s0t, ts, ts)
            krs = misc[:, ROPE_LANE0:ROPE_LANE0 + MLA_ROPE].reshape(bs, ts, MLA_ROPE)
            q4 = q.reshape(bs, ts, MLA_HEADS, HEAD_PAD)
            qrope = jnp.swapaxes(q4[..., ROPE_LANE0:ROPE_LANE0 + MLA_ROPE], 1, 2)
            qrope = qrope.reshape(bs, MLA_HEADS * ts, MLA_ROPE)
            qrr = jnp.concatenate([qrope, qrope[..., HALF_ROPE:], -qrope[..., :HALF_ROPE]], axis=-1)
            ob = _paged_attn(j, pt_flat, n_pages, r3(q), qrr, r3(ckvn), jnp.swapaxes(krs, 1, 2), pair_new,
                             pair_past, w, cache_ckv, cache_krt, n_pg)
            ys = _even_back(ys, oa.reshape(bs * ts, GLA_VW), ga, ob.reshape(bs * ts, MLA_VW), gb, w, tm_s)
            outs['gla_s'].append(jnp.swapaxes(st, -1, -2))
            outs['ckv_s'].append(ckvn.reshape(bs, ts, KV_LORA))
            outs['kr_s'].append(krs)
        else:
            w = _pack_odd(j, ln_odd, w_in_odd, conv_w, conv_b, rg_w_a, rg_b_a, rg_w_x, rg_b_x, rg_lambda,
                          w_out_odd)
            y3, hl, cv = _odd_prompt(yp.reshape(bp, tp, d), w, tt_odd)
            yp = y3.reshape(bp * tp, d)
            outs['rh_p'].append(hl.reshape(bp, D_RNN))
            outs['rc_p'].append(cv)
            ext = jnp.concatenate([jnp.zeros((bs, ts - CONV_W, D_RNN), F32), state_rglru_h[j][:, None, :],
                                   state_rglru_conv[j]], axis=1).reshape(bs * ts, D_RNN)
            ys, hfull, ufull = _odd_sample(ys, ext, w, tm_s)
            outs['rh_s'].append(hfull.reshape(bs, ts, D_RNN)[:, ts - 1])
            outs['rc_s'].append(ufull.reshape(bs, ts, D_RNN)[:, ts - (CONV_W - 1):])
    st_ = lambda name: jnp.stack(outs[name])
    return (yp.reshape(bp, tp, d), ys.reshape(bs, ts, d),
            st_('gla_p'), st_('ckv_p'), st_('kr_p'), st_('rh_p'), st_('rc_p'),
            st_('gla_s'), st_('ckv_s'), st_('kr_s'), st_('rh_s'), st_('rc_s'))
```

```python
import functools

import jax
import jax.numpy as jnp
from jax import lax
from jax.experimental import pallas as pl
from jax.experimental.pallas import tpu as pltpu

F32 = jnp.float32
BF16 = jnp.bfloat16
EPS = 1e-6

GLA_HEADS = 4
GLA_DK = 64
GLA_DV = 128
GLA_RANK = 16
GLA_TAU = 16.0
GLA_CHUNK = 64
GLA_QK = GLA_HEADS * GLA_DK
GLA_VW = GLA_HEADS * GLA_DV
MLA_HEADS = 8
MLA_NOPE = 64
MLA_ROPE = 32
MLA_V = 64
MLA_QK = MLA_NOPE + MLA_ROPE
MLA_VW = MLA_HEADS * MLA_V
MLA_NOPEW = MLA_HEADS * MLA_NOPE
Q_LORA = 384
KV_LORA = 256
ROPE_THETA = 10000.0
PAGE_SIZE = 128
D_RNN = 1280
RNN_BLOCKS = 10
RNN_BW = D_RNN // RNN_BLOCKS
CONV_W = 4
LRU_C = 8.0

LANES = 128
SUBLANES = 8
HEAD_PAD = LANES
MLA_PADW = MLA_HEADS * HEAD_PAD
HALF_ROPE = MLA_ROPE // 2
ROPE_LANE0 = MLA_NOPE
MISC_W = LANES
LOG2E = 1.4426950408889634
Q_SCALE = MLA_QK ** -0.5 * LOG2E
FLASH_HEADS_PER_STEP = 4
EXP_CLAMP = 80.0
NEG_BIG = -1e30
VMEM_LIMIT = 56 * 1024 * 1024
PAGED_KEYS_PER_STEP = 4096

NT_DIMS = (((1,), (1,)), ((), ()))
TN_DIMS = (((0,), (0,)), ((), ()))


def _dot(a, b):
    return jnp.dot(a, b, preferred_element_type=F32)


def _dot_nt(a, b):
    return lax.dot_general(a, b, NT_DIMS, preferred_element_type=F32)


def _dot_tn(a, b):
    return lax.dot_general(a, b, TN_DIMS, preferred_element_type=F32)


def _rms(x, g, n=None):
    n = x.shape[-1] if n is None else n
    ss = jnp.sum(x * x, axis=-1, keepdims=True) * (1.0 / n)
    return x * lax.rsqrt(ss + EPS) * g


def _half_silu(gh):
    return gh * jnp.tanh(gh) + gh


def _log_sigmoid(x):
    return jnp.minimum(x, 0.0) - jnp.log1p(jnp.exp(-jnp.abs(x)))


def _params(sem):
    return pltpu.CompilerParams(dimension_semantics=sem, vmem_limit_bytes=VMEM_LIMIT)


def _full(shape):
    nd = len(shape)
    return pl.BlockSpec(shape, lambda *_: (0,) * nd)


def _resident(shape):
    nd = len(shape)
    return pl.BlockSpec(shape, lambda *_: (0,) * nd, pipeline_mode=pl.Buffered(1))


def _even_front_kernel(x_ref, ln_ref, win_ref, wf2_ref, bf_ref, qn_ref, kvn_ref, wuq_ref, wuqr_ref, wuk_ref,
                       wuv_ref, qh_ref, qhs_ref, kh_ref, khs_ref, c_ref, s_ref,
                       qa_ref, ka_ref, va_ref, lf_ref, ga_ref, gb_ref, q_ref, k_ref, v_ref,
                       ckv_ref, misc_ref):
    x = x_ref[...]
    xn = _rms(x, ln_ref[...])
    z = _dot(xn.astype(BF16), win_ref[...])
    o = 0
    qa_ref[...] = z[:, o:o + GLA_QK] * (GLA_DK ** -0.5); o += GLA_QK
    ka_ref[...] = z[:, o:o + GLA_QK]; o += GLA_QK
    va_ref[...] = z[:, o:o + GLA_VW]; o += GLA_VW
    ga_ref[...] = _half_silu(z[:, o:o + GLA_VW]); o += GLA_VW
    cq = z[:, o:o + Q_LORA]; o += Q_LORA
    ckv = z[:, o:o + KV_LORA]; o += KV_LORA
    gb_ref[...] = _half_silu(z[:, o:o + MLA_VW]); o += MLA_VW
    misc = z[:, o:o + MISC_W]; o += MISC_W
    misc_rot = z[:, o:o + MISC_W]
    misc_ref[...] = misc

    f = _dot(misc.astype(BF16), wf2_ref[...]) + bf_ref[...]
    lf_ref[...] = _log_sigmoid(f) * (1.0 / GLA_TAU)

    c = c_ref[...]
    s = s_ref[...]

    cqb = _rms(cq, qn_ref[...]).astype(BF16)
    qraw = _dot(cqb, wuq_ref[...])
    qrot = _dot(cqb, wuqr_ref[...])
    gc = qh_ref[...] * c * Q_SCALE
    gs = qhs_ref[...] * s * Q_SCALE
    for h in range(MLA_HEADS):
        hs = slice(h * HEAD_PAD, (h + 1) * HEAD_PAD)
        t = qraw[:, hs]
        r = lax.rsqrt(jnp.sum(t * t, axis=-1, keepdims=True) * (1.0 / MLA_QK) + EPS)
        q_ref[:, hs] = ((t * gc + qrot[:, hs] * gs) * r).astype(BF16)

    ckvn = _rms(ckv, kvn_ref[...])
    ckv_ref[...] = ckvn
    cb = ckvn.astype(BF16)
    kraw = _dot(cb, wuk_ref[...])
    lane = lax.broadcasted_iota(jnp.int32, misc.shape, 1)
    krt = jnp.where(lane >= ROPE_LANE0, misc, 0.0)
    gc = kh_ref[...] * c
    rot_gs = misc_rot * (khs_ref[...] * s)
    for h in range(MLA_HEADS):
        hs = slice(h * HEAD_PAD, (h + 1) * HEAD_PAD)
        t = kraw[:, hs] + krt
        r = lax.rsqrt(jnp.sum(t * t, axis=-1, keepdims=True) * (1.0 / MLA_QK) + EPS)
        k_ref[:, hs] = ((t * gc + rot_gs) * r).astype(BF16)
    v_ref[...] = _dot(cb, wuv_ref[...]).astype(BF16)


def _even_front(x2, w, tabs, tm, rows_per_seq):
    n, d = x2.shape
    nblk_per_seq = rows_per_seq // tm if rows_per_seq >= tm else None
    row = lambda w_: pl.BlockSpec((tm, w_), lambda i: (i, 0))
    if nblk_per_seq is not None:
        tab = pl.BlockSpec((tm, LANES), lambda i: (i % nblk_per_seq, 0))
    else:
        tab = pl.BlockSpec((tm, LANES), lambda i: (0, 0))
    ins = [x2, w['ln'], w['win'], w['wf2'], w['bf'], w['qn'], w['kvn'], w['wuq'], w['wuqr'], w['wukp'],
           w['wuv'], w['qh'], w['qhs'], w['kh'], w['khs'], tabs[0], tabs[1]]
    in_specs = [row(d)] + [_resident(a.shape) for a in ins[1:15]] + [tab, tab]
    outs = [(GLA_QK, F32), (GLA_QK, F32), (GLA_VW, F32), (GLA_QK, F32), (GLA_VW, F32), (MLA_VW, F32),
            (MLA_PADW, BF16), (MLA_PADW, BF16), (MLA_VW, BF16), (KV_LORA, F32), (MISC_W, F32)]
    return pl.pallas_call(
        _even_front_kernel,
        grid=(n // tm,),
        in_specs=in_specs,
        out_specs=[row(w_) for w_, _ in outs],
        out_shape=[jax.ShapeDtypeStruct((n, w_), dt) for w_, dt in outs],
        compiler_params=_params(("parallel",)),
        name="even_front",
    )(*ins)


def _gla_kernel(*refs, chunk, n_chunks, n_seq, has_s0):
    if has_s0:
        q_ref, k_ref, v_ref, g_ref, s0_ref, o_ref, st_ref, s_scr = refs
    else:
        q_ref, k_ref, v_ref, g_ref, o_ref, st_ref, s_scr = refs
    tb = pl.program_id(1)

    @pl.when(tb == 0)
    def _():
        if has_s0:
            s_scr[...] = s0_ref[...]
        else:
            s_scr[...] = jnp.zeros_like(s_scr)

    c_ = chunk
    row = lax.broadcasted_iota(jnp.int32, (c_, c_), 0)
    col = lax.broadcasted_iota(jnp.int32, (c_, c_), 1)
    tri = row >= col
    tri_bf = jnp.where(tri, 1.0, 0.0).astype(BF16)
    mid_row = c_ // 2 - 1
    for c in range(n_chunks):
        sl = slice(c * c_, (c + 1) * c_)
        for b_ in range(n_seq):
            g = g_ref[b_, sl, :]
            g_hi = g.astype(BF16)
            g_lo = (g - g_hi.astype(F32)).astype(BF16)
            cum = _dot(tri_bf, g_hi) + _dot(tri_bf, g_lo)
            tot = cum[c_ - 1:c_, :]
            mid = cum[mid_row:mid_row + 1, :]
            q = q_ref[b_, sl, :]
            k = k_ref[b_, sl, :]
            qd = (q * jnp.exp(cum)).astype(BF16)
            qm = (q * jnp.exp(jnp.minimum(cum - mid, EXP_CLAMP))).astype(BF16)
            km = (k * jnp.exp(jnp.minimum(mid - cum, EXP_CLAMP))).astype(BF16)
            kd = (k * jnp.exp(tot - cum)).astype(BF16)
            etot = jnp.exp(tot)
            for h in range(GLA_HEADS):
                hs = slice(h * GLA_DK, (h + 1) * GLA_DK)
                vs = slice(h * GLA_DV, (h + 1) * GLA_DV)
                a = _dot_nt(qm[:, hs], km[:, hs])
                a = jnp.where(tri, a, 0.0).astype(BF16)
                vh = v_ref[b_, sl, vs].astype(BF16)
                st = s_scr[b_, h]
                o_ref[b_, sl, vs] = _dot_nt(qd[:, hs], st.astype(BF16)) + _dot(a, vh)
                s_scr[b_, h] = st * etot[:, hs] + _dot_tn(vh, kd[:, hs])

    @pl.when(tb == pl.num_programs(1) - 1)
    def _():
        st_ref[...] = s_scr[...]


def _gla_scan(qa, ka, va, lf, s0t, chunk, tblk, n_seq):
    b, t, _ = qa.shape
    n_chunks = tblk // chunk
    has_s0 = s0t is not None
    seq = lambda w_: pl.BlockSpec((n_seq, tblk, w_), lambda i, j: (i, j, 0))
    st_spec = pl.BlockSpec((n_seq, GLA_HEADS, GLA_DV, GLA_DK), lambda i, j: (i, 0, 0, 0))
    ins = [qa, ka, va, lf] + ([s0t] if has_s0 else [])
    in_specs = [seq(GLA_QK), seq(GLA_QK), seq(GLA_VW), seq(GLA_QK)] + ([st_spec] if has_s0 else [])
    return pl.pallas_call(
        functools.partial(_gla_kernel, chunk=chunk, n_chunks=n_chunks, n_seq=n_seq, has_s0=has_s0),
        grid=(b // n_seq, t // tblk),
        in_specs=in_specs,
        out_specs=[seq(GLA_VW), st_spec],
        out_shape=[jax.ShapeDtypeStruct((b, t, GLA_VW), F32),
                   jax.ShapeDtypeStruct((b, GLA_HEADS, GLA_DV, GLA_DK), F32)],
        scratch_shapes=[pltpu.VMEM((n_seq, GLA_HEADS, GLA_DV, GLA_DK), F32)],
        compiler_params=_params(("parallel", "arbitrary")),
        name="gla_scan",
    )(*ins)


def _flash_kernel(q_ref, k_ref, v_ref, o_ref, m_scr, l_scr, acc_scr, *, tq, n_heads):
    qi = pl.program_id(2)
    m_scr[...] = jnp.full_like(m_scr, NEG_BIG)
    l_scr[...] = jnp.zeros_like(l_scr)
    acc_scr[...] = jnp.zeros_like(acc_scr)
    lane = lax.broadcasted_iota(jnp.int32, (tq, LANES), 1)

    def block(ki, diagonal):
        ks = pl.ds(pl.multiple_of(ki * tq, tq), tq)
        for pr in range(n_heads // 2):
            v = v_ref[0, ks, pr * LANES:(pr + 1) * LANES]
            acc = acc_scr[pr]
            for hh in range(2):
                h = 2 * pr + hh
                qh = q_ref[0, :, h * HEAD_PAD:(h + 1) * HEAD_PAD]
                kh = k_ref[0, ks, h * HEAD_PAD:(h + 1) * HEAD_PAD]
                s = _dot_nt(qh, kh)
                if diagonal:
                    row = lax.broadcasted_iota(jnp.int32, (tq, tq), 0)
                    col = lax.broadcasted_iota(jnp.int32, (tq, tq), 1)
                    s = jnp.where(col <= row, s, NEG_BIG)
                m_prev = m_scr[h]
                m_new = jnp.maximum(m_prev, jnp.max(s, axis=-1, keepdims=True))
                alpha = jnp.exp2(m_prev - m_new)
                p = jnp.exp2(s - m_new[:, 0:1])
                l_scr[h] = alpha * l_scr[h] + jnp.sum(p, axis=-1, keepdims=True)
                m_scr[h] = m_new
                pv = _dot(p.astype(BF16), v)
                mine = (lane >= hh * MLA_V) & (lane < (hh + 1) * MLA_V)
                acc = jnp.where(mine, alpha * acc + pv, acc)
            acc_scr[pr] = acc

    def body(ki, carry):
        block(ki, False)
        return carry

    lax.fori_loop(0, qi, body, 0)
    block(qi, True)
    for pr in range(n_heads // 2):
        l = jnp.where(lane < MLA_V, l_scr[2 * pr], l_scr[2 * pr + 1])
        o_ref[0, :, pr * LANES:(pr + 1) * LANES] = acc_scr[pr] / l


def _flash_attn(q, k, v, tq, n_heads):
    b, t, _ = q.shape
    nq = t // tq
    ng = MLA_HEADS // n_heads
    return pl.pallas_call(
        functools.partial(_flash_kernel, tq=tq, n_heads=n_heads),
        grid=(b, ng, nq),
        in_specs=[pl.BlockSpec((1, tq, n_heads * HEAD_PAD), lambda bi, g, i: (bi, i, g)),
                  pl.BlockSpec((1, t, n_heads * HEAD_PAD), lambda bi, g, i: (bi, 0, g)),
                  pl.BlockSpec((1, t, n_heads * MLA_V), lambda bi, g, i: (bi, 0, g))],
        out_specs=pl.BlockSpec((1, tq, n_heads * MLA_V), lambda bi, g, i: (bi, i, g)),
        out_shape=jax.ShapeDtypeStruct((b, t, MLA_VW), F32),
        scratch_shapes=[pltpu.VMEM((n_heads, tq, LANES), F32), pltpu.VMEM((n_heads, tq, LANES), F32),
                        pltpu.VMEM((n_heads // 2, tq, LANES), F32)],
        compiler_params=_params(("parallel", "parallel", "arbitrary")),
        name="flash_attn",
    )(q, k, v)


def _paged_kernel(pt_ref, q_ref, qrr_ref, latn_ref, krnt_ref, cosn_ref, sinn_ref, cos_ref, sin_ref,
                  khg_ref, grc_ref, wukp_ref, wukt_ref, wuv_ref, ckv_hbm, krt_hbm,
                  o_ref, lat_buf, krt_buf, sem, lhs_scr, m_scr, l_scr, acc_scr, *, layer, n_pg, t_new):
    p_idx = pl.program_id(1)
    steps = pl.num_programs(1)
    step = pl.program_id(0) * steps + p_idx
    total = pl.num_programs(0) * steps
    slot = lax.rem(step, 2)
    nq = MLA_HEADS * t_new
    tt = n_pg * PAGE_SIZE

    def page_copies(st, sl):
        cps = []
        for g in range(n_pg):
            page = pt_ref[st * n_pg + g]
            rows = pl.ds(g * PAGE_SIZE, PAGE_SIZE)
            cps.append(pltpu.make_async_copy(ckv_hbm.at[layer, page], lat_buf.at[sl, rows, :], sem.at[sl, 0]))
            cps.append(pltpu.make_async_copy(krt_hbm.at[layer, page], krt_buf.at[sl, :, rows], sem.at[sl, 1]))
        return cps

    @pl.when(step == 0)
    def _():
        for cp in page_copies(0, 0):
            cp.start()

    @pl.when(step + 1 < total)
    def _():
        for cp in page_copies(step + 1, 1 - slot):
            cp.start()

    for cp in page_copies(step, slot):
        cp.wait()

    @pl.when(p_idx == 0)
    def _():
        qt = q_ref[0].astype(F32)
        qrep = jnp.concatenate([qt] * MLA_HEADS, axis=0)
        r_ = lax.broadcasted_iota(jnp.int32, qrep.shape, 0) // t_new
        c_ = lax.broadcasted_iota(jnp.int32, qrep.shape, 1) // HEAD_PAD
        qbd = jnp.where(r_ == c_, qrep * khg_ref[...], 0.0)
        lhs_scr[0:nq, :] = _dot_nt(qbd.astype(BF16), wukp_ref[...]).astype(BF16)
        lhs_scr[nq:, :] = wukt_ref[...]
        m_scr[...] = jnp.full_like(m_scr, NEG_BIG)
        l_scr[...] = jnp.zeros_like(l_scr)
        acc_scr[...] = jnp.zeros_like(acc_scr)

    grc = grc_ref[...]
    qrr = qrr_ref[0]

    def attend(cb, kt, cost, sint, mask):
        n = cb.shape[0]
        big = _dot_nt(lhs_scr[...], cb)
        k2 = jnp.sum(kt * kt, axis=0, keepdims=True)
        rows = []
        for h in range(MLA_HEADS):
            blk = big[nq + h * MLA_NOPE:nq + (h + 1) * MLA_NOPE]
            ssq = jnp.sum(blk * blk, axis=0, keepdims=True) + k2
            rows.append(jnp.broadcast_to(lax.rsqrt(ssq * (1.0 / MLA_QK) + EPS), (t_new, n)))
        rinv = jnp.concatenate(rows, axis=0)
        x = kt * grc
        xx = jnp.concatenate([x * cost, x * sint], axis=0).astype(BF16)
        s = (big[0:nq] + _dot(qrr, xx)) * rinv
        if mask is not None:
            s = jnp.where(mask, s, NEG_BIG)
        m_prev = m_scr[...]
        m_new = jnp.maximum(m_prev, jnp.max(s, axis=-1, keepdims=True))
        alpha = jnp.exp2(m_prev - m_new)
        p = jnp.exp2(s - m_new[:, 0:1])
        l_scr[...] = alpha * l_scr[...] + jnp.sum(p, axis=-1, keepdims=True)
        m_scr[...] = m_new
        acc_scr[...] = alpha[:, 0:1] * acc_scr[...] + _dot(p.astype(BF16), cb)

    ks = pl.ds(pl.multiple_of(p_idx * tt, tt), tt)
    attend(lat_buf[slot].astype(BF16), krt_buf[slot], cos_ref[:, ks], sin_ref[:, ks], None)

    @pl.when(p_idx == pl.num_programs(1) - 1)
    def _():
        qi = lax.broadcasted_iota(jnp.int32, (nq, t_new), 0) % t_new
        kj = lax.broadcasted_iota(jnp.int32, (nq, t_new), 1)
        attend(latn_ref[0].astype(BF16), krnt_ref[0], cosn_ref[...], sinn_ref[...], kj <= qi)
        out_lat = acc_scr[...] / l_scr[:, 0:1]
        er = lax.broadcasted_iota(jnp.int32, (nq, MLA_VW), 0) // t_new
        ec = lax.broadcasted_iota(jnp.int32, (nq, MLA_VW), 1) // MLA_V
        full = jnp.where(er == ec, _dot(out_lat.astype(BF16), wuv_ref[...]), 0.0)
        o = full[0:t_new]
        for h in range(1, MLA_HEADS):
            o = o + full[h * t_new:(h + 1) * t_new]
        o_ref[0] = o


def _paged_attn(layer, page_table_flat, n_pages, q3, qrr, latn, krnt, tabs_new, tabs_past, w,
                cache_ckv, cache_krt, n_pg):
    b, t_new, _ = q3.shape
    nq = MLA_HEADS * t_new
    steps = n_pages // n_pg
    tt = n_pg * PAGE_SIZE

    def per_b(shape):
        nd = len(shape)
        return pl.BlockSpec((1,) + shape[1:], lambda bi, p, pt: (bi,) + (0,) * (nd - 1))

    def const(shape):
        nd = len(shape)
        return pl.BlockSpec(shape, lambda bi, p, pt: (0,) * nd, pipeline_mode=pl.Buffered(1))

    hbm = pl.BlockSpec(memory_space=pl.ANY)
    ins = [q3, qrr, latn, krnt, tabs_new[0], tabs_new[1], tabs_past[0], tabs_past[1],
           w['khg'], w['grc'], w['wukp'], w['wukt'], w['wuv'], cache_ckv, cache_krt]
    in_specs = [per_b(a.shape) for a in ins[:4]] + [const(a.shape) for a in ins[4:13]] + [hbm, hbm]
    grid_spec = pltpu.PrefetchScalarGridSpec(
        num_scalar_prefetch=1,
        grid=(b, steps),
        in_specs=in_specs,
        out_specs=pl.BlockSpec((1, t_new, MLA_VW), lambda bi, p, pt: (bi, 0, 0)),
        scratch_shapes=[pltpu.VMEM((2, tt, KV_LORA), F32), pltpu.VMEM((2, MLA_ROPE, tt), F32),
                        pltpu.SemaphoreType.DMA((2, 2)),
                        pltpu.VMEM((nq + MLA_NOPEW, KV_LORA), BF16), pltpu.VMEM((nq, LANES), F32),
                        pltpu.VMEM((nq, LANES), F32), pltpu.VMEM((nq, KV_LORA), F32)],
    )
    return pl.pallas_call(
        functools.partial(_paged_kernel, layer=layer, n_pg=n_pg, t_new=t_new),
        grid_spec=grid_spec,
        out_shape=jax.ShapeDtypeStruct((b, t_new, MLA_VW), F32),
        compiler_params=_params(("arbitrary", "arbitrary")),
        name="paged_attn",
    )(page_table_flat, *ins)


def _even_back_kernel(x_ref, oa_ref, ga_ref, ob_ref, gb_ref, on_ref, wa_ref, wb_ref, y_ref):
    oa = oa_ref[...]
    on = on_ref[...]
    parts = []
    for h in range(GLA_HEADS):
        parts.append(_rms(oa[:, h * GLA_DV:(h + 1) * GLA_DV], on))
    a = jnp.concatenate(parts, axis=-1) * ga_ref[...]
    b_ = ob_ref[...] * gb_ref[...]
    y_ref[...] = x_ref[...] + _dot(a.astype(BF16), wa_ref[...]) + _dot(b_.astype(BF16), wb_ref[...])


def _even_back(x2, oa, ga, ob, gb, w, tm):
    n, d = x2.shape
    row = lambda w_: pl.BlockSpec((tm, w_), lambda i: (i, 0))
    ins = [x2, oa, ga, ob, gb, w['on'], w['wout_a'], w['wout_b']]
    return pl.pallas_call(
        _even_back_kernel,
        grid=(n // tm,),
        in_specs=[row(d), row(GLA_VW), row(GLA_VW), row(MLA_VW), row(MLA_VW)] + [_full(a.shape) for a in ins[5:]],
        out_specs=row(d),
        out_shape=jax.ShapeDtypeStruct((n, d), F32),
        compiler_params=_params(("parallel",)),
        name="even_back",
    )(*ins)


def _group_scan(a, b):
    rows, width = a.shape
    a = a.reshape(rows // SUBLANES, SUBLANES, width)
    b = b.reshape(rows // SUBLANES, SUBLANES, width)
    t = lax.broadcasted_iota(jnp.int32, a.shape, 1)
    for s in (1, 2, 4):
        keep = t >= s
        ar = pltpu.roll(a, s, 1)
        br = pltpu.roll(b, s, 1)
        b = jnp.where(keep, a * br + b, b)
        a = jnp.where(keep, a * ar, a)
    return a.reshape(rows, width), b.reshape(rows, width)


def _odd_gates(xc, wax_ref, ba_ref, bx_ref, lam_ref):
    xcb = xc.astype(BF16)
    rs, is_ = [], []
    for n in range(RNN_BLOCKS):
        ga = _dot(xcb[:, n * RNN_BW:(n + 1) * RNN_BW], wax_ref[n])
        rs.append(ga[:, :RNN_BW])
        is_.append(ga[:, RNN_BW:])
    tr = jnp.tanh(jnp.concatenate(rs, axis=-1) + ba_ref[...])
    ti = jnp.tanh(jnp.concatenate(is_, axis=-1) + bx_ref[...])
    lam = lam_ref[...]
    sp_neg_lam = jnp.maximum(-lam, 0.0) + jnp.log1p(jnp.exp(-jnp.abs(lam)))
    c1 = (-0.5 * LRU_C) * sp_neg_lam
    a = jnp.exp(c1 * tr + c1)
    b = jnp.exp2(0.5 * jnp.log2(1.0 - a * a)) * xc * (0.5 * ti + 0.5)
    return a, b


def _odd_prompt_kernel(x_ref, ln_ref, wu_ref, wg_ref, cw_ref, cb_ref, wax_ref, ba_ref, bx_ref, lam_ref,
                       wout_ref, y_ref, hl_ref, cv_ref, ush_scr, hc_scr, *, tt):
    tb = pl.program_id(1)
    ns = SUBLANES

    @pl.when(tb == 0)
    def _():
        ush_scr[0:ns, :] = jnp.zeros((ns, D_RNN), F32)
        hc_scr[...] = jnp.zeros_like(hc_scr)

    x = x_ref[0]
    xb = _rms(x, ln_ref[...]).astype(BF16)
    u = _dot(xb, wu_ref[...])
    gate = _dot(xb, wg_ref[...])
    ush_scr[ns:ns + tt, :] = u
    cw = cw_ref[...]
    xc = cb_ref[...] + u * cw[CONV_W - 1:CONV_W, :]
    for s in range(1, CONV_W):
        xc = xc + ush_scr[ns - s:ns - s + tt, :] * cw[CONV_W - 1 - s:CONV_W - s, :]
    ush_scr[0:ns, :] = ush_scr[tt:tt + ns, :]
    a, b = _odd_gates(xc, wax_ref, ba_ref, bx_ref, lam_ref)
    a, b = _group_scan(a, b)
    c = hc_scr[...]
    hs = []
    for g in range(tt // ns):
        hg = a[g * ns:(g + 1) * ns] * c + b[g * ns:(g + 1) * ns]
        hs.append(hg)
        c = hg[ns - 1:ns, :]
    hc_scr[...] = c
    h = jnp.concatenate(hs, axis=0)
    y_ref[0] = x + _dot((h * _half_silu(gate)).astype(BF16), wout_ref[...])

    @pl.when(tb == pl.num_programs(1) - 1)
    def _():
        hl_ref[0] = c
        cv_ref[0] = ush_scr[ns - (CONV_W - 1):ns, :]


def _odd_sample_kernel(x_ref, ext_ref, ln_ref, wu_ref, wg_ref, cw_ref, cb_ref, wax_ref, ba_ref, bx_ref, lam_ref,
                       wout_ref, y_ref, h_ref, u_ref):
    ns = SUBLANES
    x = x_ref[...]
    rows = x.shape[0]
    xb = _rms(x, ln_ref[...]).astype(BF16)
    u = _dot(xb, wu_ref[...])
    gate = _dot(xb, wg_ref[...])
    u_ref[...] = u
    ext = ext_ref[...]
    t = lax.broadcasted_iota(jnp.int32, (rows, D_RNN), 0) % ns
    cw = cw_ref[...]
    xc = cb_ref[...] + u * cw[CONV_W - 1:CONV_W, :]
    for s in range(1, CONV_W):
        prev = jnp.where(t < s, pltpu.roll(ext, rows + s - ns, 0), pltpu.roll(u, s, 0))
        xc = xc + prev * cw[CONV_W - 1 - s:CONV_W - s, :]
    a, b = _odd_gates(xc, wax_ref, ba_ref, bx_ref, lam_ref)
    h0_at0 = jnp.where(t == 0, pltpu.roll(ext, rows - (ns - CONV_W), 0), 0.0)
    _, h = _group_scan(a, b + a * h0_at0)
    h_ref[...] = h
    y_ref[...] = x + _dot((h * _half_silu(gate)).astype(BF16), wout_ref[...])


_ODD_WEIGHTS = ('ln', 'wu', 'wg', 'cw', 'cb', 'wax', 'ba', 'bx', 'lam', 'wout')


def _odd_prompt(x, w, tt):
    b, t, d = x.shape
    wl = [w[n] for n in _ODD_WEIGHTS]
    return pl.pallas_call(
        functools.partial(_odd_prompt_kernel, tt=tt),
        grid=(b, t // tt),
        in_specs=[pl.BlockSpec((1, tt, d), lambda i, j: (i, j, 0))] + [_resident(a.shape) for a in wl],
        out_specs=[pl.BlockSpec((1, tt, d), lambda i, j: (i, j, 0)),
                   pl.BlockSpec((1, 1, D_RNN), lambda i, j: (i, 0, 0)),
                   pl.BlockSpec((1, CONV_W - 1, D_RNN), lambda i, j: (i, 0, 0))],
        out_shape=[jax.ShapeDtypeStruct((b, t, d), F32), jax.ShapeDtypeStruct((b, 1, D_RNN), F32),
                   jax.ShapeDtypeStruct((b, CONV_W - 1, D_RNN), F32)],
        scratch_shapes=[pltpu.VMEM((tt + SUBLANES, D_RNN), F32), pltpu.VMEM((1, D_RNN), F32)],
        compiler_params=_params(("parallel", "arbitrary")),
        name="odd_prompt",
    )(x, *wl)


def _odd_sample(x2, ext, w, rows):
    n, d = x2.shape
    wl = [w[n_] for n_ in _ODD_WEIGHTS]
    row = lambda w_: pl.BlockSpec((rows, w_), lambda i: (i, 0))
    return pl.pallas_call(
        _odd_sample_kernel,
        grid=(n // rows,),
        in_specs=[row(d), row(D_RNN)] + [_resident(a.shape) for a in wl],
        out_specs=[row(d), row(D_RNN), row(D_RNN)],
        out_shape=[jax.ShapeDtypeStruct((n, d), F32), jax.ShapeDtypeStruct((n, D_RNN), F32),
                   jax.ShapeDtypeStruct((n, D_RNN), F32)],
        compiler_params=_params(("parallel",)),
        name="odd_sample",
    )(x2, ext, *wl)


def _pad_heads(v, n_used):
    lead = v.shape[:-1]
    v = v.reshape(lead + (MLA_HEADS, n_used))
    v = jnp.pad(v, [(0, 0)] * len(lead) + [(0, 0), (0, HEAD_PAD - n_used)])
    return v.reshape(lead + (MLA_PADW,))


def _rot_half(w):
    return jnp.concatenate([-w[..., HALF_ROPE:], w[..., :HALF_ROPE]], axis=-1)


def _swap_half_gain(g):
    return jnp.concatenate([g[:MLA_NOPE], g[MLA_NOPE + HALF_ROPE:], g[MLA_NOPE:MLA_NOPE + HALF_ROPE]])


def _pack_even(j, ln_even, w_in_even, gla_w_f2, gla_b_f, gla_out_norm, mla_q_norm, mla_kv_norm, mla_w_uq,
               mla_w_ukv, mla_qh_norm, mla_kh_norm, w_out_even):
    wi = w_in_even[j]
    d = wi.shape[0]
    o = 0
    seg = {}
    for name, width in (('qa', GLA_QK), ('ka', GLA_QK), ('va', GLA_VW), ('ga', GLA_VW), ('fa', GLA_RANK),
                        ('cq', Q_LORA), ('ckv', KV_LORA), ('kr', MLA_ROPE), ('gb', MLA_VW)):
        seg[name] = wi[:, o:o + width]
        o += width
    z = lambda n: jnp.zeros((d, n), wi.dtype)
    tail = MISC_W - ROPE_LANE0 - MLA_ROPE
    misc = jnp.concatenate([seg['fa'], z(ROPE_LANE0 - GLA_RANK), seg['kr'], z(tail)], 1)
    misc_rot = jnp.concatenate([z(ROPE_LANE0), _rot_half(seg['kr']), z(tail)], 1)
    win = jnp.concatenate([seg['qa'], seg['ka'], seg['va'], 0.5 * seg['ga'], seg['cq'], seg['ckv'],
                           0.5 * seg['gb'], misc, misc_rot], 1)
    wf2 = jnp.pad(gla_w_f2[j], ((0, MISC_W - GLA_RANK), (0, 0)))
    ukv = mla_w_ukv[j].reshape(KV_LORA, MLA_HEADS, MLA_NOPE + MLA_V)
    wuk = ukv[:, :, :MLA_NOPE].reshape(KV_LORA, MLA_NOPEW)
    wuv = ukv[:, :, MLA_NOPE:].reshape(KV_LORA, MLA_VW)
    uq = mla_w_uq[j].reshape(Q_LORA, MLA_HEADS, MLA_QK)
    uq_rot = jnp.concatenate([jnp.zeros_like(uq[..., :MLA_NOPE]), _rot_half(uq[..., MLA_NOPE:])], -1)
    khp = jnp.pad(mla_kh_norm[j], (0, HEAD_PAD - MLA_QK))
    return {
        'ln': ln_even[j][None, :],
        'win': win.astype(BF16),
        'wf2': wf2.astype(BF16),
        'bf': gla_b_f[j][None, :],
        'qn': mla_q_norm[j][None, :],
        'kvn': mla_kv_norm[j][None, :],
        'wuq': _pad_heads(mla_w_uq[j], MLA_QK).astype(BF16),
        'wuqr': _pad_heads(uq_rot.reshape(Q_LORA, MLA_HEADS * MLA_QK), MLA_QK).astype(BF16),
        'wukp': _pad_heads(wuk, MLA_NOPE).astype(BF16),
        'wukt': wuk.T.astype(BF16),
        'wuv': wuv.astype(BF16),
        'qh': jnp.pad(mla_qh_norm[j], (0, HEAD_PAD - MLA_QK))[None, :],
        'qhs': jnp.pad(_swap_half_gain(mla_qh_norm[j]), (0, HEAD_PAD - MLA_QK))[None, :],
        'kh': khp[None, :],
        'khs': jnp.pad(_swap_half_gain(mla_kh_norm[j]), (0, HEAD_PAD - MLA_QK))[None, :],
        'khg': jnp.tile(khp, MLA_HEADS)[None, :],
        'grc': mla_kh_norm[j][MLA_NOPE:][:, None],
        'on': gla_out_norm[j][None, :],
        'wout_a': w_out_even[j][:GLA_VW].astype(BF16),
        'wout_b': w_out_even[j][GLA_VW:].astype(BF16),
    }


def _pack_odd(j, ln_odd, w_in_odd, conv_w, conv_b, rg_w_a, rg_b_a, rg_w_x, rg_b_x, rg_lambda, w_out_odd):
    return {
        'ln': ln_odd[j][None, :],
        'wu': w_in_odd[j][:, :D_RNN].astype(BF16),
        'wg': (0.5 * w_in_odd[j][:, D_RNN:]).astype(BF16),
        'cw': conv_w[j],
        'cb': conv_b[j][None, :],
        'wax': (0.5 * jnp.concatenate([rg_w_a[j], rg_w_x[j]], axis=-1)).astype(BF16),
        'ba': 0.5 * rg_b_a[j][None, :],
        'bx': 0.5 * rg_b_x[j][None, :],
        'lam': rg_lambda[j][None, :],
        'wout': w_out_odd[j].astype(BF16),
    }


def _rope_angles(pos):
    inv_freq = ROPE_THETA ** (-jnp.arange(HALF_ROPE, dtype=F32) / HALF_ROPE)
    ang = pos.astype(F32)[:, None] * inv_freq[None, :]
    return jnp.cos(ang), jnp.sin(ang)


def _rope_tile_tables(pos):
    cos, sin = _rope_angles(pos)
    n = pos.shape[0]
    tail = LANES - ROPE_LANE0 - MLA_ROPE
    c = jnp.concatenate([jnp.ones((n, ROPE_LANE0), F32), cos, cos, jnp.ones((n, tail), F32)], 1)
    s = jnp.concatenate([jnp.zeros((n, ROPE_LANE0), F32), sin, sin, jnp.zeros((n, tail), F32)], 1)
    return c, s


def _rope_pair_tables_t(pos):
    cos, sin = _rope_angles(pos)
    return jnp.concatenate([cos, cos], 1).T, jnp.concatenate([sin, sin], 1).T


def _pick(n, prefs):
    for p in prefs:
        if n % p == 0:
            return p
    return n


def kernel(x_prompt, x_sample, state_gla, cache_ckv, cache_kr, state_rglru_h, state_rglru_conv, page_table,
           ln_even, w_in_even, gla_w_f2, gla_b_f, gla_out_norm, mla_q_norm, mla_kv_norm, mla_w_uq, mla_w_ukv,
           mla_qh_norm, mla_kh_norm, w_out_even, ln_odd, w_in_odd, conv_w, conv_b, rg_w_a, rg_b_a, rg_w_x,
           rg_b_x, rg_lambda, w_out_odd):
    bp, tp, d = x_prompt.shape
    bs, ts, _ = x_sample.shape
    n_pages = page_table.shape[1]
    past_len = n_pages * PAGE_SIZE
    depth = ln_even.shape[0] + ln_odd.shape[0]
    assert ts == SUBLANES, "sample group is handled as one 8-row group per sequence"
    assert MLA_NOPE == MLA_V

    tm_p = _pick(tp, (256, 128, 64, 32, 16, 8))
    tm_s = _pick(bs * ts, (256, 128, 64, 32, 16, 8))
    nseq_p = _pick(bp, (2, 1))
    nseq_s = _pick(bs, (8, 4, 2, 1))
    chunk_p = min(GLA_CHUNK, tp)
    tblk_p = _pick(tp, (256, 128, 64)) if tp >= GLA_CHUNK else tp
    tq = _pick(tp, (512, 256, 128))
    tt_odd = _pick(tp, (256, 128, 64))
    n_pg = _pick(n_pages, (PAGED_KEYS_PER_STEP // PAGE_SIZE, 8, 4, 2, 1))

    tabs_p = _rope_tile_tables(jnp.arange(tp))
    pos_s = past_len + jnp.arange(ts)
    tabs_s = tuple(jnp.tile(a, (tm_s // ts, 1)) for a in _rope_tile_tables(pos_s))
    pair_new = _rope_pair_tables_t(pos_s)
    pair_past = _rope_pair_tables_t(jnp.arange(past_len))
    pt_flat = page_table.reshape(-1).astype(jnp.int32)
    cache_krt = jnp.swapaxes(cache_kr, -1, -2)

    yp = x_prompt.reshape(bp * tp, d)
    ys = x_sample.reshape(bs * ts, d)
    outs = {k: [] for k in ('gla_p', 'ckv_p', 'kr_p', 'rh_p', 'rc_p', 'gla_s', 'ckv_s', 'kr_s', 'rh_s', 'rc_s')}
    for layer in range(depth):
        j = layer // 2
        if layer % 2 == 0:
            w = _pack_even(j, ln_even, w_in_even, gla_w_f2, gla_b_f, gla_out_norm, mla_q_norm, mla_kv_norm,
                           mla_w_uq, mla_w_ukv, mla_qh_norm, mla_kh_norm, w_out_even)
            qa, ka, va, lf, ga, gb, q, k, v, ckvn, misc = _even_front(yp, w, tabs_p, tm_p, tp)
            r3 = lambda a: a.reshape(bp, tp, a.shape[-1])
            oa, st = _gla_scan(r3(qa), r3(ka), r3(va), r3(lf), None, chunk_p, tblk_p, nseq_p)
            ob = _flash_attn(r3(q), r3(k), r3(v), tq, FLASH_HEADS_PER_STEP)
            yp = _even_back(yp, oa.reshape(bp * tp, GLA_VW), ga, ob.reshape(bp * tp, MLA_VW), gb, w, tm_p)
            outs['gla_p'].append(jnp.swapaxes(st, -1, -2))
            outs['ckv_p'].append(ckvn.reshape(bp, tp, KV_LORA))
            outs['kr_p'].append(misc[:, ROPE_LANE0:ROPE_LANE0 + MLA_ROPE].reshape(bp, tp, MLA_ROPE))
            qa, ka, va, lf, ga, gb, q, k, v, ckvn, misc = _even_front(ys, w, tabs_s, tm_s, ts)
            r3 = lambda a: a.reshape(bs, ts, a.shape[-1])
            s0t = jnp.swapaxes(state_gla[j], -1, -2)
            oa, st = _gla_scan(r3(qa), r3(ka), r3(va), r3(lf), s0t, ts, ts, nseq_s)
            krs = misc[:, ROPE_LANE0:ROPE_LANE0 + MLA_ROPE].reshape(bs, ts, MLA_ROPE)
            q4 = q.reshape(bs, ts, MLA_HEADS, HEAD_PAD)
            qrope = jnp.swapaxes(q4[..., ROPE_LANE0:ROPE_LANE0 + MLA_ROPE], 1, 2)
            qrope = qrope.reshape(bs, MLA_HEADS * ts, MLA_ROPE)
            qrr = jnp.concatenate([qrope, qrope[..., HALF_ROPE:], -qrope[..., :HALF_ROPE]], axis=-1)
            ob = _paged_attn(j, pt_flat, n_pages, r3(q), qrr, r3(ckvn), jnp.swapaxes(krs, 1, 2), pair_new,
                             pair_past, w, cache_ckv, cache_krt, n_pg)
            ys = _even_back(ys, oa.reshape(bs * ts, GLA_VW), ga, ob.reshape(bs * ts, MLA_VW), gb, w, tm_s)
            outs['gla_s'].append(jnp.swapaxes(st, -1, -2))
            outs['ckv_s'].append(ckvn.reshape(bs, ts, KV_LORA))
            outs['kr_s'].append(krs)
        else:
            w = _pack_odd(j, ln_odd, w_in_odd, conv_w, conv_b, rg_w_a, rg_b_a, rg_w_x, rg_b_x, rg_lambda,
                          w_out_odd)
            y3, hl, cv = _odd_prompt(yp.reshape(bp, tp, d), w, tt_odd)
            yp = y3.reshape(bp * tp, d)
            outs['rh_p'].append(hl.reshape(bp, D_RNN))
            outs['rc_p'].append(cv)
            ext = jnp.concatenate([jnp.zeros((bs, ts - CONV_W, D_RNN), F32), state_rglru_h[j][:, None, :],
                                   state_rglru_conv[j]], axis=1).reshape(bs * ts, D_RNN)
            ys, hfull, ufull = _odd_sample(ys, ext, w, tm_s)
            outs['rh_s'].append(hfull.reshape(bs, ts, D_RNN)[:, ts - 1])
            outs['rc_s'].append(ufull.reshape(bs, ts, D_RNN)[:, ts - (CONV_W - 1):])
    st_ = lambda name: jnp.stack(outs[name])
    return (yp.reshape(bp, tp, d), ys.reshape(bs, ts, d),
            st_('gla_p'), st_('ckv_p'), st_('kr_p'), st_('rh_p'), st_('rc_p'),
            st_('gla_s'), st_('ckv_s'), st_('kr_s'), st_('rh_s'), st_('rc_s'))
```

```python
import functools

import jax
import jax.numpy as jnp
from jax import lax
from jax.experimental import pallas as pl
from jax.experimental.pallas import tpu as pltpu

F32 = jnp.float32
BF16 = jnp.bfloat16
EPS = 1e-6

GLA_HEADS = 4
GLA_DK = 64
GLA_DV = 128
GLA_RANK = 16
GLA_TAU = 16.0
GLA_CHUNK = 64
GLA_QK = GLA_HEADS * GLA_DK
GLA_VW = GLA_HEADS * GLA_DV
MLA_HEADS = 8
MLA_NOPE = 64
MLA_ROPE = 32
MLA_V = 64
MLA_QK = MLA_NOPE + MLA_ROPE
MLA_VW = MLA_HEADS * MLA_V
MLA_NOPEW = MLA_HEADS * MLA_NOPE
Q_LORA = 384
KV_LORA = 256
ROPE_THETA = 10000.0
PAGE_SIZE = 128
D_RNN = 1280
RNN_BLOCKS = 10
RNN_BW = D_RNN // RNN_BLOCKS
CONV_W = 4
LRU_C = 8.0

LANES = 128
SUBLANES = 8
HEAD_PAD = LANES
MLA_PADW = MLA_HEADS * HEAD_PAD
HALF_ROPE = MLA_ROPE // 2
ROPE_LANE0 = MLA_NOPE
MISC_W = LANES
LOG2E = 1.4426950408889634
Q_SCALE = MLA_QK ** -0.5 * LOG2E
FLASH_HEADS_PER_STEP = 4
FLASH_TK = 512
EXP_CLAMP = 80.0
NEG_BIG = -1e30
VMEM_LIMIT = 56 * 1024 * 1024
PAGED_KEYS_PER_STEP = 8192

NT_DIMS = (((1,), (1,)), ((), ()))
TN_DIMS = (((0,), (0,)), ((), ()))


def _dot(a, b):
    return jnp.dot(a, b, preferred_element_type=F32)


def _dot_nt(a, b):
    return lax.dot_general(a, b, NT_DIMS, preferred_element_type=F32)


def _dot_tn(a, b):
    return lax.dot_general(a, b, TN_DIMS, preferred_element_type=F32)


def _rms(x, g, n=None):
    n = x.shape[-1] if n is None else n
    ss = jnp.sum(x * x, axis=-1, keepdims=True) * (1.0 / n)
    return x * lax.rsqrt(ss + EPS) * g


def _half_silu(gh):
    return gh * jnp.tanh(gh) + gh


def _log_sigmoid(x):
    return jnp.minimum(x, 0.0) - jnp.log1p(jnp.exp(-jnp.abs(x)))


def _params(sem, flags=None):
    return pltpu.CompilerParams(dimension_semantics=sem, vmem_limit_bytes=VMEM_LIMIT, flags=flags)


def _full(shape):
    nd = len(shape)
    return pl.BlockSpec(shape, lambda *_: (0,) * nd)


def _resident(shape):
    nd = len(shape)
    return pl.BlockSpec(shape, lambda *_: (0,) * nd, pipeline_mode=pl.Buffered(1))


def _even_front_kernel(x_ref, ln_ref, win_ref, wf2_ref, bf_ref, qn_ref, kvn_ref, wuq_ref, wuqr_ref, wuk_ref,
                       wuv_ref, qh_ref, qhs_ref, kh_ref, khs_ref, c_ref, s_ref,
                       qa_ref, ka_ref, va_ref, lf_ref, ga_ref, gb_ref, q_ref, k_ref, v_ref,
                       ckv_ref, misc_ref):
    x = x_ref[...]
    xn = _rms(x, ln_ref[...])
    z = _dot(xn.astype(BF16), win_ref[...])
    o = 0
    qa_ref[...] = z[:, o:o + GLA_QK] * (GLA_DK ** -0.5); o += GLA_QK
    ka_ref[...] = z[:, o:o + GLA_QK]; o += GLA_QK
    va_ref[...] = z[:, o:o + GLA_VW]; o += GLA_VW
    ga_ref[...] = _half_silu(z[:, o:o + GLA_VW]); o += GLA_VW
    cq = z[:, o:o + Q_LORA]; o += Q_LORA
    ckv = z[:, o:o + KV_LORA]; o += KV_LORA
    gb_ref[...] = _half_silu(z[:, o:o + MLA_VW]); o += MLA_VW
    misc = z[:, o:o + MISC_W]; o += MISC_W
    misc_rot = z[:, o:o + MISC_W]
    misc_ref[...] = misc

    f = _dot(misc.astype(BF16), wf2_ref[...]) + bf_ref[...]
    lf_ref[...] = _log_sigmoid(f) * (1.0 / GLA_TAU)

    c = c_ref[...]
    s = s_ref[...]

    cqb = _rms(cq, qn_ref[...]).astype(BF16)
    qraw = _dot(cqb, wuq_ref[...])
    qrot = _dot(cqb, wuqr_ref[...])
    gc = qh_ref[...] * c * Q_SCALE
    gs = qhs_ref[...] * s * Q_SCALE
    for h in range(MLA_HEADS):
        hs = slice(h * HEAD_PAD, (h + 1) * HEAD_PAD)
        t = qraw[:, hs]
        r = lax.rsqrt(jnp.sum(t * t, axis=-1, keepdims=True) * (1.0 / MLA_QK) + EPS)
        q_ref[:, hs] = ((t * gc + qrot[:, hs] * gs) * r).astype(BF16)

    ckvn = _rms(ckv, kvn_ref[...])
    ckv_ref[...] = ckvn
    cb = ckvn.astype(BF16)
    kraw = _dot(cb, wuk_ref[...])
    lane = lax.broadcasted_iota(jnp.int32, misc.shape, 1)
    krt = jnp.where(lane >= ROPE_LANE0, misc, 0.0)
    gc = kh_ref[...] * c
    rot_gs = misc_rot * (khs_ref[...] * s)
    for h in range(MLA_HEADS):
        hs = slice(h * HEAD_PAD, (h + 1) * HEAD_PAD)
        t = kraw[:, hs] + krt
        r = lax.rsqrt(jnp.sum(t * t, axis=-1, keepdims=True) * (1.0 / MLA_QK) + EPS)
        k_ref[:, hs] = ((t * gc + rot_gs) * r).astype(BF16)
    v_ref[...] = _dot(cb, wuv_ref[...]).astype(BF16)


def _even_front(x2, w, tabs, tm, rows_per_seq):
    n, d = x2.shape
    nblk_per_seq = rows_per_seq // tm if rows_per_seq >= tm else None
    row = lambda w_: pl.BlockSpec((tm, w_), lambda i: (i, 0))
    if nblk_per_seq is not None:
        tab = pl.BlockSpec((tm, LANES), lambda i: (i % nblk_per_seq, 0))
    else:
        tab = pl.BlockSpec((tm, LANES), lambda i: (0, 0))
    ins = [x2, w['ln'], w['win'], w['wf2'], w['bf'], w['qn'], w['kvn'], w['wuq'], w['wuqr'], w['wukp'],
           w['wuv'], w['qh'], w['qhs'], w['kh'], w['khs'], tabs[0], tabs[1]]
    in_specs = [row(d)] + [_resident(a.shape) for a in ins[1:15]] + [tab, tab]
    outs = [(GLA_QK, F32), (GLA_QK, F32), (GLA_VW, F32), (GLA_QK, F32), (GLA_VW, F32), (MLA_VW, F32),
            (MLA_PADW, BF16), (MLA_PADW, BF16), (MLA_VW, BF16), (KV_LORA, F32), (MISC_W, F32)]
    return pl.pallas_call(
        _even_front_kernel,
        grid=(n // tm,),
        in_specs=in_specs,
        out_specs=[row(w_) for w_, _ in outs],
        out_shape=[jax.ShapeDtypeStruct((n, w_), dt) for w_, dt in outs],
        compiler_params=_params(("parallel",)),
        name="even_front",
    )(*ins)


def _gla_kernel(*refs, chunk, n_chunks, n_seq, has_s0):
    if has_s0:
        q_ref, k_ref, v_ref, g_ref, s0_ref, o_ref, st_ref, s_scr = refs
    else:
        q_ref, k_ref, v_ref, g_ref, o_ref, st_ref, s_scr = refs
    tb = pl.program_id(1)

    @pl.when(tb == 0)
    def _():
        if has_s0:
            s_scr[...] = s0_ref[...]
        else:
            s_scr[...] = jnp.zeros_like(s_scr)

    c_ = chunk
    row = lax.broadcasted_iota(jnp.int32, (c_, c_), 0)
    col = lax.broadcasted_iota(jnp.int32, (c_, c_), 1)
    tri = row >= col
    tri_bf = jnp.where(tri, 1.0, 0.0).astype(BF16)
    mid_row = c_ // 2 - 1
    lane_head = lax.broadcasted_iota(jnp.int32, (c_, LANES), 1) // GLA_DK
    zero_bf = jnp.zeros((c_, LANES), BF16)
    for c in range(n_chunks):
        sl = slice(c * c_, (c + 1) * c_)
        for b_ in range(n_seq):
            g = g_ref[b_, sl, :]
            g_hi = g.astype(BF16)
            g_lo = (g - g_hi.astype(F32)).astype(BF16)
            cum = _dot(tri_bf, g_hi) + _dot(tri_bf, g_lo)
            tot = cum[c_ - 1:c_, :]
            mid = cum[mid_row:mid_row + 1, :]
            q = q_ref[b_, sl, :]
            k = k_ref[b_, sl, :]
            qd = (q * jnp.exp(cum)).astype(BF16)
            qm = (q * jnp.exp(jnp.minimum(cum - mid, EXP_CLAMP))).astype(BF16)
            km = (k * jnp.exp(jnp.minimum(mid - cum, EXP_CLAMP))).astype(BF16)
            kd = (k * jnp.exp(tot - cum)).astype(BF16)
            etot = jnp.exp(tot)
            for pr in range(GLA_HEADS // 2):
                ts = slice(pr * LANES, (pr + 1) * LANES)
                st = s_scr[b_, pr]
                st_bf = st.astype(BF16)
                kdm, vhs = [], []
                for hh in range(2):
                    h = 2 * pr + hh
                    mine = lane_head == hh
                    vs = slice(h * GLA_DV, (h + 1) * GLA_DV)
                    vh = v_ref[b_, sl, vs].astype(BF16)
                    a = _dot_nt(jnp.where(mine, qm[:, ts], zero_bf), km[:, ts])
                    a = jnp.where(tri, a, 0.0).astype(BF16)
                    qdm = jnp.where(mine, qd[:, ts], zero_bf)
                    if c_ % GLA_DK == 0:
                        o_ref[b_, sl, vs] = _dot(jnp.concatenate([qdm, a], axis=1),
                                                 jnp.concatenate([st_bf, vh], axis=0))
                    else:
                        o_ref[b_, sl, vs] = _dot(qdm, st_bf) + _dot(a, vh)
                    kdm.append(jnp.where(mine, kd[:, ts], zero_bf))
                    vhs.append(vh)
                upd = _dot_tn(jnp.concatenate(kdm, axis=0), jnp.concatenate(vhs, axis=0))
                decay = jnp.transpose(jnp.broadcast_to(etot[:, ts], (LANES, LANES)))
                s_scr[b_, pr] = st * decay + upd

    @pl.when(tb == pl.num_programs(1) - 1)
    def _():
        st_ref[...] = s_scr[...]


def _gla_scan(qa, ka, va, lf, s0t, chunk, tblk, n_seq):
    b, t, _ = qa.shape
    n_chunks = tblk // chunk
    has_s0 = s0t is not None
    seq = lambda w_: pl.BlockSpec((n_seq, tblk, w_), lambda i, j: (i, j, 0))
    st_spec = pl.BlockSpec((n_seq, GLA_HEADS // 2, 2 * GLA_DK, GLA_DV), lambda i, j: (i, 0, 0, 0))
    ins = [qa, ka, va, lf] + ([s0t] if has_s0 else [])
    in_specs = [seq(GLA_QK), seq(GLA_QK), seq(GLA_VW), seq(GLA_QK)] + ([st_spec] if has_s0 else [])
    return pl.pallas_call(
        functools.partial(_gla_kernel, chunk=chunk, n_chunks=n_chunks, n_seq=n_seq, has_s0=has_s0),
        grid=(b // n_seq, t // tblk),
        in_specs=in_specs,
        out_specs=[seq(GLA_VW), st_spec],
        out_shape=[jax.ShapeDtypeStruct((b, t, GLA_VW), F32),
                   jax.ShapeDtypeStruct((b, GLA_HEADS // 2, 2 * GLA_DK, GLA_DV), F32)],
        scratch_shapes=[pltpu.VMEM((n_seq, GLA_HEADS // 2, 2 * GLA_DK, GLA_DV), F32)],
        compiler_params=_params(("parallel", "arbitrary")),
        name="gla_scan",
    )(*ins)


def _flash_kernel(q_ref, k_ref, v_ref, o_ref, m_scr, l_scr, acc_scr, *, tq, tk, n_heads):
    qi = pl.program_id(2)
    m_scr[...] = jnp.full_like(m_scr, NEG_BIG)
    l_scr[...] = jnp.zeros_like(l_scr)
    acc_scr[...] = jnp.zeros_like(acc_scr)
    lane = lax.broadcasted_iota(jnp.int32, (tq, LANES), 1)

    def block(ki, diagonal):
        ks = pl.ds(pl.multiple_of(ki * tk, tk), tk)
        for pr in range(n_heads // 2):
            v = v_ref[0, ks, pr * LANES:(pr + 1) * LANES]
            acc = acc_scr[pr]
            for hh in range(2):
                h = 2 * pr + hh
                qh = q_ref[0, :, h * HEAD_PAD:(h + 1) * HEAD_PAD]
                kh = k_ref[0, ks, h * HEAD_PAD:(h + 1) * HEAD_PAD]
                s = _dot_nt(qh, kh)
                if diagonal:
                    row = lax.broadcasted_iota(jnp.int32, (tq, tk), 0) + qi * tq
                    col = lax.broadcasted_iota(jnp.int32, (tq, tk), 1) + ki * tk
                    s = jnp.where(col <= row, s, NEG_BIG)
                m_prev = m_scr[h]
                m_new = jnp.maximum(m_prev, jnp.max(s, axis=-1, keepdims=True))
                alpha = jnp.exp2(m_prev - m_new)
                p = jnp.exp2(s - m_new[:, 0:1])
                l_scr[h] = alpha * l_scr[h] + jnp.sum(p, axis=-1, keepdims=True)
                m_scr[h] = m_new
                pv = _dot(p.astype(BF16), v)
                mine = (lane >= hh * MLA_V) & (lane < (hh + 1) * MLA_V)
                acc = jnp.where(mine, alpha * acc + pv, acc)
            acc_scr[pr] = acc

    def body(ki, carry):
        block(ki, False)
        return carry

    per_q = tq // tk
    lax.fori_loop(0, qi * per_q, body, 0)
    for j in range(per_q):
        block(qi * per_q + j, True)
    for pr in range(n_heads // 2):
        l = jnp.where(lane < MLA_V, l_scr[2 * pr], l_scr[2 * pr + 1])
        o_ref[0, :, pr * LANES:(pr + 1) * LANES] = acc_scr[pr] / l


def _flash_attn(q, k, v, tq, n_heads):
    b, t, _ = q.shape
    nq = t // tq
    ng = MLA_HEADS // n_heads
    return pl.pallas_call(
        functools.partial(_flash_kernel, tq=tq, tk=min(FLASH_TK, tq), n_heads=n_heads),
        grid=(b, ng, nq),
        in_specs=[pl.BlockSpec((1, tq, n_heads * HEAD_PAD), lambda bi, g, i: (bi, i, g)),
                  pl.BlockSpec((1, t, n_heads * HEAD_PAD), lambda bi, g, i: (bi, 0, g)),
                  pl.BlockSpec((1, t, n_heads * MLA_V), lambda bi, g, i: (bi, 0, g))],
        out_specs=pl.BlockSpec((1, tq, n_heads * MLA_V), lambda bi, g, i: (bi, i, g)),
        out_shape=jax.ShapeDtypeStruct((b, t, MLA_VW), F32),
        scratch_shapes=[pltpu.VMEM((n_heads, tq, LANES), F32), pltpu.VMEM((n_heads, tq, LANES), F32),
                        pltpu.VMEM((n_heads // 2, tq, LANES), F32)],
        compiler_params=_params(("parallel", "parallel", "arbitrary")),
        name="flash_attn",
    )(q, k, v)


def _paged_kernel(pt_ref, q_ref, qrr_ref, latn_ref, krnt_ref, cosn_ref, sinn_ref, cos_ref, sin_ref,
                  khg_ref, grc_ref, wukp_ref, wukt_ref, wuv_ref, ckv_hbm, krt_hbm,
                  o_ref, lat_buf, krt_buf, sem, lhs_scr, m_scr, l_scr, acc_scr, *, layer, n_pg, t_new):
    p_idx = pl.program_id(1)
    steps = pl.num_programs(1)
    step = pl.program_id(0) * steps + p_idx
    total = pl.num_programs(0) * steps
    slot = lax.rem(step, 2)
    nq = MLA_HEADS * t_new
    tt = n_pg * PAGE_SIZE

    def page_copies(st, sl):
        cps = []
        for g in range(n_pg):
            page = pt_ref[st * n_pg + g]
            rows = pl.ds(g * PAGE_SIZE, PAGE_SIZE)
            cps.append(pltpu.make_async_copy(ckv_hbm.at[layer, page], lat_buf.at[sl, rows, :], sem.at[sl, 0]))
            cps.append(pltpu.make_async_copy(krt_hbm.at[layer, page], krt_buf.at[sl, :, rows], sem.at[sl, 1]))
        return cps

    @pl.when(step == 0)
    def _():
        for cp in page_copies(0, 0):
            cp.start()

    @pl.when(step + 1 < total)
    def _():
        for cp in page_copies(step + 1, 1 - slot):
            cp.start()

    for cp in page_copies(step, slot):
        cp.wait()

    @pl.when(p_idx == 0)
    def _():
        qt = q_ref[0].astype(F32)
        qrep = jnp.concatenate([qt] * MLA_HEADS, axis=0)
        r_ = lax.broadcasted_iota(jnp.int32, qrep.shape, 0) // t_new
        c_ = lax.broadcasted_iota(jnp.int32, qrep.shape, 1) // HEAD_PAD
        qbd = jnp.where(r_ == c_, qrep * khg_ref[...], 0.0)
        lhs_scr[0:nq, :] = _dot_nt(qbd.astype(BF16), wukp_ref[...]).astype(BF16)
        lhs_scr[nq:, :] = wukt_ref[...]
        m_scr[...] = jnp.full_like(m_scr, NEG_BIG)
        l_scr[...] = jnp.zeros_like(l_scr)
        acc_scr[...] = jnp.zeros_like(acc_scr)

    grc = grc_ref[...]
    qrr = qrr_ref[0]

    def attend(cb, kt, cost, sint, mask):
        n = cb.shape[0]
        big = _dot_nt(lhs_scr[...], cb)
        k2 = jnp.sum(kt * kt, axis=0, keepdims=True)
        rows = []
        for h in range(MLA_HEADS):
            blk = big[nq + h * MLA_NOPE:nq + (h + 1) * MLA_NOPE]
            ssq = jnp.sum(blk * blk, axis=0, keepdims=True) + k2
            rows.append(jnp.broadcast_to(lax.rsqrt(ssq * (1.0 / MLA_QK) + EPS), (t_new, n)))
        rinv = jnp.concatenate(rows, axis=0)
        x = kt * grc
        xx = jnp.concatenate([x * cost, x * sint], axis=0).astype(BF16)
        s = (big[0:nq] + _dot(qrr, xx)) * rinv
        if mask is not None:
            s = jnp.where(mask, s, NEG_BIG)
        m_prev = m_scr[...]
        m_new = jnp.maximum(m_prev, jnp.max(s, axis=-1, keepdims=True))
        alpha = jnp.exp2(m_prev - m_new)
        p = jnp.exp2(s - m_new[:, 0:1])
        l_scr[...] = alpha * l_scr[...] + jnp.sum(p, axis=-1, keepdims=True)
        m_scr[...] = m_new
        acc_scr[...] = alpha[:, 0:1] * acc_scr[...] + _dot(p.astype(BF16), cb)

    ks = pl.ds(pl.multiple_of(p_idx * tt, tt), tt)
    attend(lat_buf[slot].astype(BF16), krt_buf[slot], cos_ref[:, ks], sin_ref[:, ks], None)

    @pl.when(p_idx == pl.num_programs(1) - 1)
    def _():
        qi = lax.broadcasted_iota(jnp.int32, (nq, t_new), 0) % t_new
        kj = lax.broadcasted_iota(jnp.int32, (nq, t_new), 1)
        attend(latn_ref[0].astype(BF16), krnt_ref[0], cosn_ref[...], sinn_ref[...], kj <= qi)
        out_lat = acc_scr[...] / l_scr[:, 0:1]
        er = lax.broadcasted_iota(jnp.int32, (nq, MLA_VW), 0) // t_new
        ec = lax.broadcasted_iota(jnp.int32, (nq, MLA_VW), 1) // MLA_V
        full = jnp.where(er == ec, _dot(out_lat.astype(BF16), wuv_ref[...]), 0.0)
        o = full[0:t_new]
        for h in range(1, MLA_HEADS):
            o = o + full[h * t_new:(h + 1) * t_new]
        o_ref[0] = o


def _paged_attn(layer, page_table_flat, n_pages, q3, qrr, latn, krnt, tabs_new, tabs_past, w,
                cache_ckv, cache_krt, n_pg):
    b, t_new, _ = q3.shape
    nq = MLA_HEADS * t_new
    steps = n_pages // n_pg
    tt = n_pg * PAGE_SIZE

    def per_b(shape):
        nd = len(shape)
        return pl.BlockSpec((1,) + shape[1:], lambda bi, p, pt: (bi,) + (0,) * (nd - 1))

    def const(shape):
        nd = len(shape)
        return pl.BlockSpec(shape, lambda bi, p, pt: (0,) * nd, pipeline_mode=pl.Buffered(1))

    hbm = pl.BlockSpec(memory_space=pl.ANY)
    ins = [q3, qrr, latn, krnt, tabs_new[0], tabs_new[1], tabs_past[0], tabs_past[1],
           w['khg'], w['grc'], w['wukp'], w['wukt'], w['wuv'], cache_ckv, cache_krt]
    in_specs = [per_b(a.shape) for a in ins[:4]] + [const(a.shape) for a in ins[4:13]] + [hbm, hbm]
    grid_spec = pltpu.PrefetchScalarGridSpec(
        num_scalar_prefetch=1,
        grid=(b, steps),
        in_specs=in_specs,
        out_specs=pl.BlockSpec((1, t_new, MLA_VW), lambda bi, p, pt: (bi, 0, 0)),
        scratch_shapes=[pltpu.VMEM((2, tt, KV_LORA), F32), pltpu.VMEM((2, MLA_ROPE, tt), F32),
                        pltpu.SemaphoreType.DMA((2, 2)),
                        pltpu.VMEM((nq + MLA_NOPEW, KV_LORA), BF16), pltpu.VMEM((nq, LANES), F32),
                        pltpu.VMEM((nq, LANES), F32), pltpu.VMEM((nq, KV_LORA), F32)],
    )
    return pl.pallas_call(
        functools.partial(_paged_kernel, layer=layer, n_pg=n_pg, t_new=t_new),
        grid_spec=grid_spec,
        out_shape=jax.ShapeDtypeStruct((b, t_new, MLA_VW), F32),
        compiler_params=_params(("arbitrary", "arbitrary")),
        name="paged_attn",
    )(page_table_flat, *ins)


def _even_back_kernel(x_ref, oa_ref, ga_ref, ob_ref, gb_ref, on_ref, wa_ref, wb_ref, y_ref):
    oa = oa_ref[...]
    on = on_ref[...]
    parts = []
    for h in range(GLA_HEADS):
        parts.append(_rms(oa[:, h * GLA_DV:(h + 1) * GLA_DV], on))
    a = jnp.concatenate(parts, axis=-1) * ga_ref[...]
    b_ = ob_ref[...] * gb_ref[...]
    y_ref[...] = x_ref[...] + _dot(a.astype(BF16), wa_ref[...]) + _dot(b_.astype(BF16), wb_ref[...])


def _even_back(x2, oa, ga, ob, gb, w, tm):
    n, d = x2.shape
    row = lambda w_: pl.BlockSpec((tm, w_), lambda i: (i, 0))
    ins = [x2, oa, ga, ob, gb, w['on'], w['wout_a'], w['wout_b']]
    return pl.pallas_call(
        _even_back_kernel,
        grid=(n // tm,),
        in_specs=[row(d), row(GLA_VW), row(GLA_VW), row(MLA_VW), row(MLA_VW)] + [_full(a.shape) for a in ins[5:]],
        out_specs=row(d),
        out_shape=jax.ShapeDtypeStruct((n, d), F32),
        compiler_params=_params(("parallel",)),
        name="even_back",
    )(*ins)


def _group_scan(a, b):
    rows, width = a.shape
    a = a.reshape(rows // SUBLANES, SUBLANES, width)
    b = b.reshape(rows // SUBLANES, SUBLANES, width)
    t = lax.broadcasted_iota(jnp.int32, a.shape, 1)
    for s in (1, 2, 4):
        keep = t >= s
        ar = pltpu.roll(a, s, 1)
        br = pltpu.roll(b, s, 1)
        b = jnp.where(keep, a * br + b, b)
        a = jnp.where(keep, a * ar, a)
    return a.reshape(rows, width), b.reshape(rows, width)


def _odd_gates(xc, wax_ref, ba_ref, bx_ref, lam_ref):
    xcb = xc.astype(BF16)
    rs, is_ = [], []
    for n in range(RNN_BLOCKS):
        ga = _dot(xcb[:, n * RNN_BW:(n + 1) * RNN_BW], wax_ref[n])
        rs.append(ga[:, :RNN_BW])
        is_.append(ga[:, RNN_BW:])
    tr = jnp.tanh(jnp.concatenate(rs, axis=-1) + ba_ref[...])
    ti = jnp.tanh(jnp.concatenate(is_, axis=-1) + bx_ref[...])
    lam = lam_ref[...]
    sp_neg_lam = jnp.maximum(-lam, 0.0) + jnp.log1p(jnp.exp(-jnp.abs(lam)))
    c1 = (-0.5 * LRU_C) * sp_neg_lam
    a = jnp.exp(c1 * tr + c1)
    b = jnp.exp2(0.5 * jnp.log2(1.0 - a * a)) * xc * (0.5 * ti + 0.5)
    return a, b


def _odd_prompt_kernel(x_ref, ln_ref, wu_ref, wg_ref, cw_ref, cb_ref, wax_ref, ba_ref, bx_ref, lam_ref,
                       wout_ref, y_ref, hl_ref, cv_ref, ush_scr, hc_scr, *, tt):
    tb = pl.program_id(1)
    ns = SUBLANES

    @pl.when(tb == 0)
    def _():
        ush_scr[0:ns, :] = jnp.zeros((ns, D_RNN), F32)
        hc_scr[...] = jnp.zeros_like(hc_scr)

    x = x_ref[0]
    xb = _rms(x, ln_ref[...]).astype(BF16)
    u = _dot(xb, wu_ref[...])
    gate = _dot(xb, wg_ref[...])
    ush_scr[ns:ns + tt, :] = u
    cw = cw_ref[...]
    xc = cb_ref[...] + u * cw[CONV_W - 1:CONV_W, :]
    for s in range(1, CONV_W):
        xc = xc + ush_scr[ns - s:ns - s + tt, :] * cw[CONV_W - 1 - s:CONV_W - s, :]
    ush_scr[0:ns, :] = ush_scr[tt:tt + ns, :]
    a, b = _odd_gates(xc, wax_ref, ba_ref, bx_ref, lam_ref)
    a, b = _group_scan(a, b)
    c = hc_scr[...]
    hs = []
    for g in range(tt // ns):
        hg = a[g * ns:(g + 1) * ns] * c + b[g * ns:(g + 1) * ns]
        hs.append(hg)
        c = hg[ns - 1:ns, :]
    hc_scr[...] = c
    h = jnp.concatenate(hs, axis=0)
    y_ref[0] = x + _dot((h * _half_silu(gate)).astype(BF16), wout_ref[...])

    @pl.when(tb == pl.num_programs(1) - 1)
    def _():
        hl_ref[0] = c
        cv_ref[0] = ush_scr[ns - (CONV_W - 1):ns, :]


def _odd_sample_kernel(x_ref, ext_ref, ln_ref, wu_ref, wg_ref, cw_ref, cb_ref, wax_ref, ba_ref, bx_ref, lam_ref,
                       wout_ref, y_ref, h_ref, u_ref):
    ns = SUBLANES
    x = x_ref[...]
    rows = x.shape[0]
    xb = _rms(x, ln_ref[...]).astype(BF16)
    u = _dot(xb, wu_ref[...])
    gate = _dot(xb, wg_ref[...])
    u_ref[...] = u
    ext = ext_ref[...]
    t = lax.broadcasted_iota(jnp.int32, (rows, D_RNN), 0) % ns
    cw = cw_ref[...]
    xc = cb_ref[...] + u * cw[CONV_W - 1:CONV_W, :]
    for s in range(1, CONV_W):
        prev = jnp.where(t < s, pltpu.roll(ext, rows + s - ns, 0), pltpu.roll(u, s, 0))
        xc = xc + prev * cw[CONV_W - 1 - s:CONV_W - s, :]
    a, b = _odd_gates(xc, wax_ref, ba_ref, bx_ref, lam_ref)
    h0_at0 = jnp.where(t == 0, pltpu.roll(ext, rows - (ns - CONV_W), 0), 0.0)
    _, h = _group_scan(a, b + a * h0_at0)
    h_ref[...] = h
    y_ref[...] = x + _dot((h * _half_silu(gate)).astype(BF16), wout_ref[...])


_ODD_WEIGHTS = ('ln', 'wu', 'wg', 'cw', 'cb', 'wax', 'ba', 'bx', 'lam', 'wout')


def _odd_prompt(x, w, tt):
    b, t, d = x.shape
    wl = [w[n] for n in _ODD_WEIGHTS]
    return pl.pallas_call(
        functools.partial(_odd_prompt_kernel, tt=tt),
        grid=(b, t // tt),
        in_specs=[pl.BlockSpec((1, tt, d), lambda i, j: (i, j, 0))] + [_resident(a.shape) for a in wl],
        out_specs=[pl.BlockSpec((1, tt, d), lambda i, j: (i, j, 0)),
                   pl.BlockSpec((1, 1, D_RNN), lambda i, j: (i, 0, 0)),
                   pl.BlockSpec((1, CONV_W - 1, D_RNN), lambda i, j: (i, 0, 0))],
        out_shape=[jax.ShapeDtypeStruct((b, t, d), F32), jax.ShapeDtypeStruct((b, 1, D_RNN), F32),
                   jax.ShapeDtypeStruct((b, CONV_W - 1, D_RNN), F32)],
        scratch_shapes=[pltpu.VMEM((tt + SUBLANES, D_RNN), F32), pltpu.VMEM((1, D_RNN), F32)],
        compiler_params=_params(("parallel", "arbitrary")),
        name="odd_prompt",
    )(x, *wl)


def _odd_sample(x2, ext, w, rows):
    n, d = x2.shape
    wl = [w[n_] for n_ in _ODD_WEIGHTS]
    row = lambda w_: pl.BlockSpec((rows, w_), lambda i: (i, 0))
    return pl.pallas_call(
        _odd_sample_kernel,
        grid=(n // rows,),
        in_specs=[row(d), row(D_RNN)] + [_resident(a.shape) for a in wl],
        out_specs=[row(d), row(D_RNN), row(D_RNN)],
        out_shape=[jax.ShapeDtypeStruct((n, d), F32), jax.ShapeDtypeStruct((n, D_RNN), F32),
                   jax.ShapeDtypeStruct((n, D_RNN), F32)],
        compiler_params=_params(("parallel",)),
        name="odd_sample",
    )(x2, ext, *wl)


def _pad_heads(v, n_used):
    lead = v.shape[:-1]
    v = v.reshape(lead + (MLA_HEADS, n_used))
    v = jnp.pad(v, [(0, 0)] * len(lead) + [(0, 0), (0, HEAD_PAD - n_used)])
    return v.reshape(lead + (MLA_PADW,))


def _rot_half(w):
    return jnp.concatenate([-w[..., HALF_ROPE:], w[..., :HALF_ROPE]], axis=-1)


def _swap_half_gain(g):
    return jnp.concatenate([g[:MLA_NOPE], g[MLA_NOPE + HALF_ROPE:], g[MLA_NOPE:MLA_NOPE + HALF_ROPE]])


def _pack_even(j, ln_even, w_in_even, gla_w_f2, gla_b_f, gla_out_norm, mla_q_norm, mla_kv_norm, mla_w_uq,
               mla_w_ukv, mla_qh_norm, mla_kh_norm, w_out_even):
    wi = w_in_even[j]
    d = wi.shape[0]
    o = 0
    seg = {}
    for name, width in (('qa', GLA_QK), ('ka', GLA_QK), ('va', GLA_VW), ('ga', GLA_VW), ('fa', GLA_RANK),
                        ('cq', Q_LORA), ('ckv', KV_LORA), ('kr', MLA_ROPE), ('gb', MLA_VW)):
        seg[name] = wi[:, o:o + width]
        o += width
    z = lambda n: jnp.zeros((d, n), wi.dtype)
    tail = MISC_W - ROPE_LANE0 - MLA_ROPE
    misc = jnp.concatenate([seg['fa'], z(ROPE_LANE0 - GLA_RANK), seg['kr'], z(tail)], 1)
    misc_rot = jnp.concatenate([z(ROPE_LANE0), _rot_half(seg['kr']), z(tail)], 1)
    win = jnp.concatenate([seg['qa'], seg['ka'], seg['va'], 0.5 * seg['ga'], seg['cq'], seg['ckv'],
                           0.5 * seg['gb'], misc, misc_rot], 1)
    wf2 = jnp.pad(gla_w_f2[j], ((0, MISC_W - GLA_RANK), (0, 0)))
    ukv = mla_w_ukv[j].reshape(KV_LORA, MLA_HEADS, MLA_NOPE + MLA_V)
    wuk = ukv[:, :, :MLA_NOPE].reshape(KV_LORA, MLA_NOPEW)
    wuv = ukv[:, :, MLA_NOPE:].reshape(KV_LORA, MLA_VW)
    uq = mla_w_uq[j].reshape(Q_LORA, MLA_HEADS, MLA_QK)
    uq_rot = jnp.concatenate([jnp.zeros_like(uq[..., :MLA_NOPE]), _rot_half(uq[..., MLA_NOPE:])], -1)
    khp = jnp.pad(mla_kh_norm[j], (0, HEAD_PAD - MLA_QK))
    return {
        'ln': ln_even[j][None, :],
        'win': win.astype(BF16),
        'wf2': wf2.astype(BF16),
        'bf': gla_b_f[j][None, :],
        'qn': mla_q_norm[j][None, :],
        'kvn': mla_kv_norm[j][None, :],
        'wuq': _pad_heads(mla_w_uq[j], MLA_QK).astype(BF16),
        'wuqr': _pad_heads(uq_rot.reshape(Q_LORA, MLA_HEADS * MLA_QK), MLA_QK).astype(BF16),
        'wukp': _pad_heads(wuk, MLA_NOPE).astype(BF16),
        'wukt': wuk.T.astype(BF16),
        'wuv': wuv.astype(BF16),
        'qh': jnp.pad(mla_qh_norm[j], (0, HEAD_PAD - MLA_QK))[None, :],
        'qhs': jnp.pad(_swap_half_gain(mla_qh_norm[j]), (0, HEAD_PAD - MLA_QK))[None, :],
        'kh': khp[None, :],
        'khs': jnp.pad(_swap_half_gain(mla_kh_norm[j]), (0, HEAD_PAD - MLA_QK))[None, :],
        'khg': jnp.tile(khp, MLA_HEADS)[None, :],
        'grc': mla_kh_norm[j][MLA_NOPE:][:, None],
        'on': gla_out_norm[j][None, :],
        'wout_a': w_out_even[j][:GLA_VW].astype(BF16),
        'wout_b': w_out_even[j][GLA_VW:].astype(BF16),
    }


def _pack_odd(j, ln_odd, w_in_odd, conv_w, conv_b, rg_w_a, rg_b_a, rg_w_x, rg_b_x, rg_lambda, w_out_odd):
    return {
        'ln': ln_odd[j][None, :],
        'wu': w_in_odd[j][:, :D_RNN].astype(BF16),
        'wg': (0.5 * w_in_odd[j][:, D_RNN:]).astype(BF16),
        'cw': conv_w[j],
        'cb': conv_b[j][None, :],
        'wax': (0.5 * jnp.concatenate([rg_w_a[j], rg_w_x[j]], axis=-1)).astype(BF16),
        'ba': 0.5 * rg_b_a[j][None, :],
        'bx': 0.5 * rg_b_x[j][None, :],
        'lam': rg_lambda[j][None, :],
        'wout': w_out_odd[j].astype(BF16),
    }


def _rope_angles(pos):
    inv_freq = ROPE_THETA ** (-jnp.arange(HALF_ROPE, dtype=F32) / HALF_ROPE)
    ang = pos.astype(F32)[:, None] * inv_freq[None, :]
    return jnp.cos(ang), jnp.sin(ang)


def _rope_tile_tables(pos):
    cos, sin = _rope_angles(pos)
    n = pos.shape[0]
    tail = LANES - ROPE_LANE0 - MLA_ROPE
    c = jnp.concatenate([jnp.ones((n, ROPE_LANE0), F32), cos, cos, jnp.ones((n, tail), F32)], 1)
    s = jnp.concatenate([jnp.zeros((n, ROPE_LANE0), F32), sin, sin, jnp.zeros((n, tail), F32)], 1)
    return c, s


def _rope_pair_tables_t(pos):
    cos, sin = _rope_angles(pos)
    return jnp.concatenate([cos, cos], 1).T, jnp.concatenate([sin, sin], 1).T


def _pick(n, prefs):
    for p in prefs:
        if n % p == 0:
            return p
    return n


def kernel(x_prompt, x_sample, state_gla, cache_ckv, cache_kr, state_rglru_h, state_rglru_conv, page_table,
           ln_even, w_in_even, gla_w_f2, gla_b_f, gla_out_norm, mla_q_norm, mla_kv_norm, mla_w_uq, mla_w_ukv,
           mla_qh_norm, mla_kh_norm, w_out_even, ln_odd, w_in_odd, conv_w, conv_b, rg_w_a, rg_b_a, rg_w_x,
           rg_b_x, rg_lambda, w_out_odd):
    bp, tp, d = x_prompt.shape
    bs, ts, _ = x_sample.shape
    n_pages = page_table.shape[1]
    past_len = n_pages * PAGE_SIZE
    depth = ln_even.shape[0] + ln_odd.shape[0]
    assert ts == SUBLANES, "sample group is handled as one 8-row group per sequence"
    assert MLA_NOPE == MLA_V

    tm_p = _pick(tp, (256, 128, 64, 32, 16, 8))
    tm_s = _pick(bs * ts, (256, 128, 64, 32, 16, 8))
    tm_back = _pick(tp, (512, 256, 128, 64, 32, 16, 8))
    nseq_p = _pick(bp, (2, 1))
    nseq_s = _pick(bs, (8, 4, 2, 1))
    chunk_p = min(GLA_CHUNK, tp)
    tblk_p = _pick(tp, (256, 128, 64)) if tp >= GLA_CHUNK else tp
    tq = _pick(tp, (512, 256, 128))
    tt_odd = _pick(tp, (256, 128, 64))
    n_pg = _pick(n_pages, (PAGED_KEYS_PER_STEP // PAGE_SIZE, 32, 16, 8, 4, 2, 1))

    tabs_p = _rope_tile_tables(jnp.arange(tp))
    pos_s = past_len + jnp.arange(ts)
    tabs_s = tuple(jnp.tile(a, (tm_s // ts, 1)) for a in _rope_tile_tables(pos_s))
    pair_new = _rope_pair_tables_t(pos_s)
    pair_past = _rope_pair_tables_t(jnp.arange(past_len))
    pt_flat = page_table.reshape(-1).astype(jnp.int32)
    cache_krt = jnp.swapaxes(cache_kr, -1, -2)

    yp = x_prompt.reshape(bp * tp, d)
    ys = x_sample.reshape(bs * ts, d)
    outs = {k: [] for k in ('gla_p', 'ckv_p', 'kr_p', 'rh_p', 'rc_p', 'gla_s', 'ckv_s', 'kr_s', 'rh_s', 'rc_s')}
    for layer in range(depth):
        j = layer // 2
        if layer % 2 == 0:
            w = _pack_even(j, ln_even, w_in_even, gla_w_f2, gla_b_f, gla_out_norm, mla_q_norm, mla_kv_norm,
                           mla_w_uq, mla_w_ukv, mla_qh_norm, mla_kh_norm, w_out_even)
            qa, ka, va, lf, ga, gb, q, k, v, ckvn, misc = _even_front(yp, w, tabs_p, tm_p, tp)
            r3 = lambda a: a.reshape(bp, tp, a.shape[-1])
            oa, st = _gla_scan(r3(qa), r3(ka), r3(va), r3(lf), None, chunk_p, tblk_p, nseq_p)
            ob = _flash_attn(r3(q), r3(k), r3(v), tq, FLASH_HEADS_PER_STEP)
            yp = _even_back(yp, oa.reshape(bp * tp, GLA_VW), ga, ob.reshape(bp * tp, MLA_VW), gb, w, tm_back)
            outs['gla_p'].append(st.reshape(bp, GLA_HEADS, GLA_DK, GLA_DV))
            outs['ckv_p'].append(ckvn.reshape(bp, tp, KV_LORA))
            outs['kr_p'].append(misc[:, ROPE_LANE0:ROPE_LANE0 + MLA_ROPE].reshape(bp, tp, MLA_ROPE))
            qa, ka, va, lf, ga, gb, q, k, v, ckvn, misc = _even_front(ys, w, tabs_s, tm_s, ts)
            r3 = lambda a: a.reshape(bs, ts, a.shape[-1])
            s0t = state_gla[j].reshape(bs, GLA_HEADS // 2, 2 * GLA_DK, GLA_DV)
            oa, st = _gla_scan(r3(qa), r3(ka), r3(va), r3(lf), s0t, ts, ts, nseq_s)
            krs = misc[:, ROPE_LANE0:ROPE_LANE0 + MLA_ROPE].reshape(bs, ts, MLA_ROPE)
            q4 = q.reshape(bs, ts, MLA_HEADS, HEAD_PAD)
            qrope = jnp.swapaxes(q4[..., ROPE_LANE0:ROPE_LANE0 + MLA_ROPE], 1, 2)
            qrope = qrope.reshape(bs, MLA_HEADS * ts, MLA_ROPE)
            qrr = jnp.concatenate([qrope, qrope[..., HALF_ROPE:], -qrope[..., :HALF_ROPE]], axis=-1)
            ob = _paged_attn(j, pt_flat, n_pages, r3(q), qrr, r3(ckvn), jnp.swapaxes(krs, 1, 2), pair_new,
                             pair_past, w, cache_ckv, cache_krt, n_pg)
            ys = _even_back(ys, oa.reshape(bs * ts, GLA_VW), ga, ob.reshape(bs * ts, MLA_VW), gb, w, tm_s)
            outs['gla_s'].append(st.reshape(bs, GLA_HEADS, GLA_DK, GLA_DV))
            outs['ckv_s'].append(ckvn.reshape(bs, ts, KV_LORA))
            outs['kr_s'].append(krs)
        else:
            w = _pack_odd(j, ln_odd, w_in_odd, conv_w, conv_b, rg_w_a, rg_b_a, rg_w_x, rg_b_x, rg_lambda,
                          w_out_odd)
            y3, hl, cv = _odd_prompt(yp.reshape(bp, tp, d), w, tt_odd)
            yp = y3.reshape(bp * tp, d)
            outs['rh_p'].append(hl.reshape(bp, D_RNN))
            outs['rc_p'].append(cv)
            ext = jnp.concatenate([jnp.zeros((bs, ts - CONV_W, D_RNN), F32), state_rglru_h[j][:, None, :],
                                   state_rglru_conv[j]], axis=1).reshape(bs * ts, D_RNN)
            ys, hfull, ufull = _odd_sample(ys, ext, w, tm_s)
            outs['rh_s'].append(hfull.reshape(bs, ts, D_RNN)[:, ts - 1])
            outs['rc_s'].append(ufull.reshape(bs, ts, D_RNN)[:, ts - (CONV_W - 1):])
    st_ = lambda name: jnp.stack(outs[name])
    return (yp.reshape(bp, tp, d), ys.reshape(bs, ts, d),
            st_('gla_p'), st_('ckv_p'), st_('kr_p'), st_('rh_p'), st_('rc_p'),
            st_('gla_s'), st_('ckv_s'), st_('kr_s'), st_('rh_s'), st_('rc_s'))
```

```python
import functools

import jax
import jax.numpy as jnp
from jax import lax
from jax.experimental import pallas as pl
from jax.experimental.pallas import tpu as pltpu

F32 = jnp.float32
BF16 = jnp.bfloat16
EPS = 1e-6

GLA_HEADS = 4
GLA_DK = 64
GLA_DV = 128
GLA_RANK = 16
GLA_TAU = 16.0
GLA_CHUNK = 64
GLA_QK = GLA_HEADS * GLA_DK
GLA_VW = GLA_HEADS * GLA_DV
MLA_HEADS = 8
MLA_NOPE = 64
MLA_ROPE = 32
MLA_V = 64
MLA_QK = MLA_NOPE + MLA_ROPE
MLA_VW = MLA_HEADS * MLA_V
MLA_NOPEW = MLA_HEADS * MLA_NOPE
Q_LORA = 384
KV_LORA = 256
ROPE_THETA = 10000.0
PAGE_SIZE = 128
D_RNN = 1280
RNN_BLOCKS = 10
RNN_BW = D_RNN // RNN_BLOCKS
CONV_W = 4
LRU_C = 8.0

LANES = 128
SUBLANES = 8
HEAD_PAD = LANES
MLA_PADW = MLA_HEADS * HEAD_PAD
HALF_ROPE = MLA_ROPE // 2
ROPE_LANE0 = MLA_NOPE
MISC_W = LANES
LOG2E = 1.4426950408889634
Q_SCALE = MLA_QK ** -0.5 * LOG2E
FLASH_HEADS_PER_STEP = 4
FLASH_TK = 512
ODD_SUBBLOCKS = 1
EXP_CLAMP = 80.0
NEG_BIG = -1e30
VMEM_LIMIT = 56 * 1024 * 1024
PAGED_KEYS_PER_STEP = 8192

NT_DIMS = (((1,), (1,)), ((), ()))
TN_DIMS = (((0,), (0,)), ((), ()))


def _dot(a, b):
    return jnp.dot(a, b, preferred_element_type=F32)


def _dot_nt(a, b):
    return lax.dot_general(a, b, NT_DIMS, preferred_element_type=F32)


def _dot_tn(a, b):
    return lax.dot_general(a, b, TN_DIMS, preferred_element_type=F32)


def _rms(x, g, n=None):
    n = x.shape[-1] if n is None else n
    ss = jnp.sum(x * x, axis=-1, keepdims=True) * (1.0 / n)
    return x * lax.rsqrt(ss + EPS) * g


def _half_silu(gh):
    return gh * jnp.tanh(gh) + gh


def _log_sigmoid(x):
    return jnp.minimum(x, 0.0) - jnp.log1p(jnp.exp(-jnp.abs(x)))


def _params(sem, flags=None):
    return pltpu.CompilerParams(dimension_semantics=sem, vmem_limit_bytes=VMEM_LIMIT, flags=flags)


def _full(shape):
    nd = len(shape)
    return pl.BlockSpec(shape, lambda *_: (0,) * nd)


def _resident(shape):
    nd = len(shape)
    return pl.BlockSpec(shape, lambda *_: (0,) * nd, pipeline_mode=pl.Buffered(1))


def _even_front_kernel(x_ref, ln_ref, win_ref, wf2_ref, bf_ref, qn_ref, kvn_ref, wuq_ref, wuqr_ref, wuk_ref,
                       wuv_ref, qh_ref, qhs_ref, kh_ref, khs_ref, c_ref, s_ref,
                       qa_ref, ka_ref, va_ref, lf_ref, ga_ref, gb_ref, q_ref, k_ref, v_ref,
                       ckv_ref, misc_ref):
    x = x_ref[...]
    xn = _rms(x, ln_ref[...])
    z = _dot(xn.astype(BF16), win_ref[...])
    o = 0
    qa_ref[...] = z[:, o:o + GLA_QK] * (GLA_DK ** -0.5); o += GLA_QK
    ka_ref[...] = z[:, o:o + GLA_QK]; o += GLA_QK
    va_ref[...] = z[:, o:o + GLA_VW]; o += GLA_VW
    ga_ref[...] = _half_silu(z[:, o:o + GLA_VW]); o += GLA_VW
    cq = z[:, o:o + Q_LORA]; o += Q_LORA
    ckv = z[:, o:o + KV_LORA]; o += KV_LORA
    gb_ref[...] = _half_silu(z[:, o:o + MLA_VW]); o += MLA_VW
    misc = z[:, o:o + MISC_W]; o += MISC_W
    misc_rot = z[:, o:o + MISC_W]
    misc_ref[...] = misc

    f = _dot(misc.astype(BF16), wf2_ref[...]) + bf_ref[...]
    lf_ref[...] = _log_sigmoid(f) * (1.0 / GLA_TAU)

    c = c_ref[...]
    s = s_ref[...]

    cqb = _rms(cq, qn_ref[...]).astype(BF16)
    ckvn = _rms(ckv, kvn_ref[...])
    ckv_ref[...] = ckvn
    cb = ckvn.astype(BF16)
    qraw = _dot(cqb, wuq_ref[...])
    qrot = _dot(cqb, wuqr_ref[...])
    kraw = _dot(cb, wuk_ref[...])
    v_ref[...] = _dot_nt(wuv_ref[...], cb).astype(BF16)
    gc = qh_ref[...] * c * Q_SCALE
    gs = qhs_ref[...] * s * Q_SCALE
    for h in range(MLA_HEADS):
        hs = slice(h * HEAD_PAD, (h + 1) * HEAD_PAD)
        t = qraw[:, hs]
        r = lax.rsqrt(jnp.sum(t * t, axis=-1, keepdims=True) * (1.0 / MLA_QK) + EPS)
        q_ref[:, hs] = ((t * gc + qrot[:, hs] * gs) * r).astype(BF16)

    lane = lax.broadcasted_iota(jnp.int32, misc.shape, 1)
    krt = jnp.where(lane >= ROPE_LANE0, misc, 0.0)
    gc = kh_ref[...] * c
    rot_gs = misc_rot * (khs_ref[...] * s)
    for h in range(MLA_HEADS):
        hs = slice(h * HEAD_PAD, (h + 1) * HEAD_PAD)
        t = kraw[:, hs] + krt
        r = lax.rsqrt(jnp.sum(t * t, axis=-1, keepdims=True) * (1.0 / MLA_QK) + EPS)
        k_ref[:, hs] = ((t * gc + rot_gs) * r).astype(BF16)


def _even_front(x2, w, tabs, tm, rows_per_seq):
    n, d = x2.shape
    nblk_per_seq = rows_per_seq // tm if rows_per_seq >= tm else None
    row = lambda w_: pl.BlockSpec((tm, w_), lambda i: (i, 0))
    if nblk_per_seq is not None:
        tab = pl.BlockSpec((tm, LANES), lambda i: (i % nblk_per_seq, 0))
    else:
        tab = pl.BlockSpec((tm, LANES), lambda i: (0, 0))
    ins = [x2, w['ln'], w['win'], w['wf2'], w['bf'], w['qn'], w['kvn'], w['wuq'], w['wuqr'], w['wukp'],
           w['wuvt'], w['qh'], w['qhs'], w['kh'], w['khs'], tabs[0], tabs[1]]
    in_specs = [row(d)] + [_resident(a.shape) for a in ins[1:15]] + [tab, tab]
    outs = [(GLA_QK, F32), (GLA_QK, F32), (GLA_VW, F32), (GLA_QK, F32), (GLA_VW, F32), (MLA_VW, F32),
            (MLA_PADW, BF16), (MLA_PADW, BF16), None, (KV_LORA, F32), (MISC_W, F32)]
    vt_spec = pl.BlockSpec((MLA_VW, tm), lambda i: (0, i))
    vt_shape = jax.ShapeDtypeStruct((MLA_VW, n), BF16)
    return pl.pallas_call(
        _even_front_kernel,
        grid=(n // tm,),
        in_specs=in_specs,
        out_specs=[vt_spec if o_ is None else row(o_[0]) for o_ in outs],
        out_shape=[vt_shape if o_ is None else jax.ShapeDtypeStruct((n, o_[0]), o_[1]) for o_ in outs],
        compiler_params=_params(("parallel",)),
        name="even_front",
    )(*ins)


def _gla_kernel(*refs, chunk, n_chunks, n_seq, has_s0):
    if has_s0:
        q_ref, k_ref, v_ref, g_ref, s0_ref, o_ref, st_ref, s_scr = refs
    else:
        q_ref, k_ref, v_ref, g_ref, o_ref, st_ref, s_scr = refs
    tb = pl.program_id(1)

    @pl.when(tb == 0)
    def _():
        if has_s0:
            s_scr[...] = s0_ref[...]
        else:
            s_scr[...] = jnp.zeros_like(s_scr)

    c_ = chunk
    row = lax.broadcasted_iota(jnp.int32, (c_, c_), 0)
    col = lax.broadcasted_iota(jnp.int32, (c_, c_), 1)
    tri = row >= col
    tri_bf = jnp.where(tri, 1.0, 0.0).astype(BF16)
    mid_row = c_ // 2 - 1
    lane_head = lax.broadcasted_iota(jnp.int32, (c_, LANES), 1) // GLA_DK
    zero_bf = jnp.zeros((c_, LANES), BF16)
    for c in range(n_chunks):
        sl = slice(c * c_, (c + 1) * c_)
        cums = []
        for b_ in range(n_seq):
            g = g_ref[b_, sl, :]
            g_hi = g.astype(BF16)
            g_lo = (g - g_hi.astype(F32)).astype(BF16)
            cums.append(_dot(tri_bf, g_hi) + _dot(tri_bf, g_lo))
        work = []
        for b_ in range(n_seq):
            cum = cums[b_]
            tot = cum[c_ - 1:c_, :]
            mid = cum[mid_row:mid_row + 1, :]
            q = q_ref[b_, sl, :]
            k = k_ref[b_, sl, :]
            qd = (q * jnp.exp(cum)).astype(BF16)
            qm = (q * jnp.exp(jnp.minimum(cum - mid, EXP_CLAMP))).astype(BF16)
            km = (k * jnp.exp(jnp.minimum(mid - cum, EXP_CLAMP))).astype(BF16)
            kd = (k * jnp.exp(tot - cum)).astype(BF16)
            etot = jnp.exp(tot)
            for pr in range(GLA_HEADS // 2):
                ts = slice(pr * LANES, (pr + 1) * LANES)
                heads = []
                for hh in range(2):
                    h = 2 * pr + hh
                    mine = lane_head == hh
                    vs = slice(h * GLA_DV, (h + 1) * GLA_DV)
                    vh = v_ref[b_, sl, vs].astype(BF16)
                    a = _dot_nt(jnp.where(mine, qm[:, ts], zero_bf), km[:, ts])
                    heads.append((vs, vh, a, jnp.where(mine, qd[:, ts], zero_bf),
                                  jnp.where(mine, kd[:, ts], zero_bf)))
                work.append((b_, pr, etot[:, ts], heads))
        upds = [_dot_tn(jnp.concatenate([hd[4] for hd in heads], axis=0),
                        jnp.concatenate([hd[1] for hd in heads], axis=0)) for _, _, _, heads in work]
        for (b_, pr, etot_t, heads), upd in zip(work, upds):
            st = s_scr[b_, pr]
            st_bf = st.astype(BF16)
            for vs, vh, a, qdm, _ in heads:
                a = jnp.where(tri, a, 0.0).astype(BF16)
                if c_ % GLA_DK == 0:
                    o_ref[b_, sl, vs] = _dot(jnp.concatenate([qdm, a], axis=1),
                                             jnp.concatenate([st_bf, vh], axis=0))
                else:
                    o_ref[b_, sl, vs] = _dot(qdm, st_bf) + _dot(a, vh)
            decay = jnp.transpose(jnp.broadcast_to(etot_t, (LANES, LANES)))
            s_scr[b_, pr] = st * decay + upd

    @pl.when(tb == pl.num_programs(1) - 1)
    def _():
        st_ref[...] = s_scr[...]


def _gla_scan(qa, ka, va, lf, s0t, chunk, tblk, n_seq):
    b, t, _ = qa.shape
    n_chunks = tblk // chunk
    has_s0 = s0t is not None
    seq = lambda w_: pl.BlockSpec((n_seq, tblk, w_), lambda i, j: (i, j, 0))
    st_spec = pl.BlockSpec((n_seq, GLA_HEADS // 2, 2 * GLA_DK, GLA_DV), lambda i, j: (i, 0, 0, 0))
    ins = [qa, ka, va, lf] + ([s0t] if has_s0 else [])
    in_specs = [seq(GLA_QK), seq(GLA_QK), seq(GLA_VW), seq(GLA_QK)] + ([st_spec] if has_s0 else [])
    return pl.pallas_call(
        functools.partial(_gla_kernel, chunk=chunk, n_chunks=n_chunks, n_seq=n_seq, has_s0=has_s0),
        grid=(b // n_seq, t // tblk),
        in_specs=in_specs,
        out_specs=[seq(GLA_VW), st_spec],
        out_shape=[jax.ShapeDtypeStruct((b, t, GLA_VW), F32),
                   jax.ShapeDtypeStruct((b, GLA_HEADS // 2, 2 * GLA_DK, GLA_DV), F32)],
        scratch_shapes=[pltpu.VMEM((n_seq, GLA_HEADS // 2, 2 * GLA_DK, GLA_DV), F32)],
        compiler_params=_params(("parallel", "arbitrary")),
        name="gla_scan",
    )(*ins)


def _flash_kernel(q_ref, k_ref, vt_ref, o_ref, m_scr, l_scr, acc_scr, *, tq, tk, n_heads):
    qi = pl.program_id(2)
    m_scr[...] = jnp.full_like(m_scr, NEG_BIG)
    l_scr[...] = jnp.zeros_like(l_scr)
    acc_scr[...] = jnp.zeros_like(acc_scr)

    def block(ki, diagonal):
        ks = pl.ds(pl.multiple_of(ki * tk, tk), tk)
        sts = []
        for h in range(n_heads):
            kh = k_ref[0, ks, h * HEAD_PAD:(h + 1) * HEAD_PAD]
            qh = q_ref[0, :, h * HEAD_PAD:(h + 1) * HEAD_PAD]
            sts.append(_dot_nt(kh, qh))
        for h in range(n_heads):
            st = sts[h]
            if diagonal:
                key = lax.broadcasted_iota(jnp.int32, (tk, tq), 0) + ki * tk
                qry = lax.broadcasted_iota(jnp.int32, (tk, tq), 1) + qi * tq
                st = jnp.where(key <= qry, st, NEG_BIG)
            m_prev = m_scr[h]
            m_new = jnp.maximum(m_prev, jnp.max(st, axis=0, keepdims=True))
            alpha = jnp.exp2(m_prev - m_new)
            p = jnp.exp2(st - m_new)
            l_scr[h] = alpha * l_scr[h] + jnp.sum(p, axis=0, keepdims=True)
            m_scr[h] = m_new
            vt = vt_ref[h * MLA_V:(h + 1) * MLA_V, ks]
            acc_scr[h] = alpha * acc_scr[h] + _dot(vt, p.astype(BF16))

    def body(ki, carry):
        block(ki, False)
        return carry

    per_q = tq // tk
    lax.fori_loop(0, qi * per_q, body, 0)
    for j in range(per_q):
        block(qi * per_q + j, True)
    for pr in range(n_heads // 2):
        ot = jnp.concatenate([acc_scr[2 * pr] / l_scr[2 * pr], acc_scr[2 * pr + 1] / l_scr[2 * pr + 1]], axis=0)
        o_ref[0, :, pr * LANES:(pr + 1) * LANES] = jnp.transpose(ot)


def _flash_attn(q, k, vt, tq, n_heads):
    b, t, _ = q.shape
    nq = t // tq
    ng = MLA_HEADS // n_heads
    return pl.pallas_call(
        functools.partial(_flash_kernel, tq=tq, tk=min(FLASH_TK, tq), n_heads=n_heads),
        grid=(b, ng, nq),
        in_specs=[pl.BlockSpec((1, tq, n_heads * HEAD_PAD), lambda bi, g, i: (bi, i, g)),
                  pl.BlockSpec((1, t, n_heads * HEAD_PAD), lambda bi, g, i: (bi, 0, g)),
                  pl.BlockSpec((n_heads * MLA_V, t), lambda bi, g, i: (g, bi))],
        out_specs=pl.BlockSpec((1, tq, n_heads * MLA_V), lambda bi, g, i: (bi, i, g)),
        out_shape=jax.ShapeDtypeStruct((b, t, MLA_VW), F32),
        scratch_shapes=[pltpu.VMEM((n_heads, 1, tq), F32), pltpu.VMEM((n_heads, 1, tq), F32),
                        pltpu.VMEM((n_heads, MLA_V, tq), F32)],
        compiler_params=_params(("parallel", "parallel", "arbitrary")),
        name="flash_attn",
    )(q, k, vt)


def _flash_kernel_rowmajor_unused(q_ref, k_ref, v_ref, o_ref, m_scr, l_scr, acc_scr, *, tq, tk, n_heads):
    qi = pl.program_id(2)
    m_scr[...] = jnp.full_like(m_scr, NEG_BIG)
    l_scr[...] = jnp.zeros_like(l_scr)
    acc_scr[...] = jnp.zeros_like(acc_scr)
    lane = lax.broadcasted_iota(jnp.int32, (tq, LANES), 1)

    def block(ki, diagonal):
        ks = pl.ds(pl.multiple_of(ki * tk, tk), tk)
        for pr in range(n_heads // 2):
            v = v_ref[0, ks, pr * LANES:(pr + 1) * LANES]
            acc = acc_scr[pr]
            for hh in range(2):
                h = 2 * pr + hh
                qh = q_ref[0, :, h * HEAD_PAD:(h + 1) * HEAD_PAD]
                kh = k_ref[0, ks, h * HEAD_PAD:(h + 1) * HEAD_PAD]
                s = _dot_nt(qh, kh)
                if diagonal:
                    row = lax.broadcasted_iota(jnp.int32, (tq, tk), 0) + qi * tq
                    col = lax.broadcasted_iota(jnp.int32, (tq, tk), 1) + ki * tk
                    s = jnp.where(col <= row, s, NEG_BIG)
                m_prev = m_scr[h]
                m_new = jnp.maximum(m_prev, jnp.max(s, axis=-1, keepdims=True))
                alpha = jnp.exp2(m_prev - m_new)
                p = jnp.exp2(s - m_new[:, 0:1])
                l_scr[h] = alpha * l_scr[h] + jnp.sum(p, axis=-1, keepdims=True)
                m_scr[h] = m_new
                pv = _dot(p.astype(BF16), v)
                mine = (lane >= hh * MLA_V) & (lane < (hh + 1) * MLA_V)
                acc = jnp.where(mine, alpha * acc + pv, acc)
            acc_scr[pr] = acc

    def body(ki, carry):
        block(ki, False)
        return carry

    per_q = tq // tk
    lax.fori_loop(0, qi * per_q, body, 0)
    for j in range(per_q):
        block(qi * per_q + j, True)
    for pr in range(n_heads // 2):
        l = jnp.where(lane < MLA_V, l_scr[2 * pr], l_scr[2 * pr + 1])
        o_ref[0, :, pr * LANES:(pr + 1) * LANES] = acc_scr[pr] / l


def _flash_attn_rowmajor_unused(q, k, v, tq, n_heads):
    b, t, _ = q.shape
    nq = t // tq
    ng = MLA_HEADS // n_heads
    return pl.pallas_call(
        functools.partial(_flash_kernel, tq=tq, tk=min(FLASH_TK, tq), n_heads=n_heads),
        grid=(b, ng, nq),
        in_specs=[pl.BlockSpec((1, tq, n_heads * HEAD_PAD), lambda bi, g, i: (bi, i, g)),
                  pl.BlockSpec((1, t, n_heads * HEAD_PAD), lambda bi, g, i: (bi, 0, g)),
                  pl.BlockSpec((1, t, n_heads * MLA_V), lambda bi, g, i: (bi, 0, g))],
        out_specs=pl.BlockSpec((1, tq, n_heads * MLA_V), lambda bi, g, i: (bi, i, g)),
        out_shape=jax.ShapeDtypeStruct((b, t, MLA_VW), F32),
        scratch_shapes=[pltpu.VMEM((n_heads, tq, LANES), F32), pltpu.VMEM((n_heads, tq, LANES), F32),
                        pltpu.VMEM((n_heads // 2, tq, LANES), F32)],
        compiler_params=_params(("parallel", "parallel", "arbitrary")),
        name="flash_attn",
    )(q, k, v)


def _paged_kernel(pt_ref, q_ref, qrr_ref, latn_ref, krnt_ref, cosn_ref, sinn_ref, cos_ref, sin_ref,
                  khg_ref, grc_ref, wukp_ref, wukt_ref, wuv_ref, ckv_hbm, krt_hbm,
                  o_ref, lat_buf, krt_buf, sem, lhs_scr, m_scr, l_scr, acc_scr, *, layer, n_pg, t_new):
    p_idx = pl.program_id(1)
    steps = pl.num_programs(1)
    step = pl.program_id(0) * steps + p_idx
    total = pl.num_programs(0) * steps
    slot = lax.rem(step, 2)
    nq = MLA_HEADS * t_new
    tt = n_pg * PAGE_SIZE

    def page_copies(st, sl):
        cps = []
        for g in range(n_pg):
            page = pt_ref[st * n_pg + g]
            rows = pl.ds(g * PAGE_SIZE, PAGE_SIZE)
            cps.append(pltpu.make_async_copy(ckv_hbm.at[layer, page], lat_buf.at[sl, rows, :], sem.at[sl, 0]))
            cps.append(pltpu.make_async_copy(krt_hbm.at[layer, page], krt_buf.at[sl, :, rows], sem.at[sl, 1]))
        return cps

    @pl.when(step == 0)
    def _():
        for cp in page_copies(0, 0):
            cp.start()

    @pl.when(step + 1 < total)
    def _():
        for cp in page_copies(step + 1, 1 - slot):
            cp.start()

    for cp in page_copies(step, slot):
        cp.wait()

    @pl.when(p_idx == 0)
    def _():
        qt = q_ref[0].astype(F32)
        qrep = jnp.concatenate([qt] * MLA_HEADS, axis=0)
        r_ = lax.broadcasted_iota(jnp.int32, qrep.shape, 0) // t_new
        c_ = lax.broadcasted_iota(jnp.int32, qrep.shape, 1) // HEAD_PAD
        qbd = jnp.where(r_ == c_, qrep * khg_ref[...], 0.0)
        lhs_scr[0:nq, :] = _dot_nt(qbd.astype(BF16), wukp_ref[...]).astype(BF16)
        lhs_scr[nq:, :] = wukt_ref[...]
        m_scr[...] = jnp.full_like(m_scr, NEG_BIG)
        l_scr[...] = jnp.zeros_like(l_scr)
        acc_scr[...] = jnp.zeros_like(acc_scr)

    grc = grc_ref[...]
    qrr = qrr_ref[0]

    def attend(cb, kt, cost, sint, mask):
        n = cb.shape[0]
        big = _dot_nt(lhs_scr[...], cb)
        k2 = jnp.sum(kt * kt, axis=0, keepdims=True)
        rows = []
        for h in range(MLA_HEADS):
            blk = big[nq + h * MLA_NOPE:nq + (h + 1) * MLA_NOPE]
            ssq = jnp.sum(blk * blk, axis=0, keepdims=True) + k2
            rows.append(jnp.broadcast_to(lax.rsqrt(ssq * (1.0 / MLA_QK) + EPS), (t_new, n)))
        rinv = jnp.concatenate(rows, axis=0)
        x = kt * grc
        xx = jnp.concatenate([x * cost, x * sint], axis=0).astype(BF16)
        s = (big[0:nq] + _dot(qrr, xx)) * rinv
        if mask is not None:
            s = jnp.where(mask, s, NEG_BIG)
        m_prev = m_scr[...]
        m_new = jnp.maximum(m_prev, jnp.max(s, axis=-1, keepdims=True))
        alpha = jnp.exp2(m_prev - m_new)
        p = jnp.exp2(s - m_new[:, 0:1])
        l_scr[...] = alpha * l_scr[...] + jnp.sum(p, axis=-1, keepdims=True)
        m_scr[...] = m_new
        acc_scr[...] = alpha[:, 0:1] * acc_scr[...] + _dot(p.astype(BF16), cb)

    ks = pl.ds(pl.multiple_of(p_idx * tt, tt), tt)
    attend(lat_buf[slot].astype(BF16), krt_buf[slot], cos_ref[:, ks], sin_ref[:, ks], None)

    @pl.when(p_idx == pl.num_programs(1) - 1)
    def _():
        qi = lax.broadcasted_iota(jnp.int32, (nq, t_new), 0) % t_new
        kj = lax.broadcasted_iota(jnp.int32, (nq, t_new), 1)
        attend(latn_ref[0].astype(BF16), krnt_ref[0], cosn_ref[...], sinn_ref[...], kj <= qi)
        out_lat = acc_scr[...] / l_scr[:, 0:1]
        er = lax.broadcasted_iota(jnp.int32, (nq, MLA_VW), 0) // t_new
        ec = lax.broadcasted_iota(jnp.int32, (nq, MLA_VW), 1) // MLA_V
        full = jnp.where(er == ec, _dot(out_lat.astype(BF16), wuv_ref[...]), 0.0)
        o = full[0:t_new]
        for h in range(1, MLA_HEADS):
            o = o + full[h * t_new:(h + 1) * t_new]
        o_ref[0] = o


def _paged_attn(layer, page_table_flat, n_pages, q3, qrr, latn, krnt, tabs_new, tabs_past, w,
                cache_ckv, cache_krt, n_pg):
    b, t_new, _ = q3.shape
    nq = MLA_HEADS * t_new
    steps = n_pages // n_pg
    tt = n_pg * PAGE_SIZE

    def per_b(shape):
        nd = len(shape)
        return pl.BlockSpec((1,) + shape[1:], lambda bi, p, pt: (bi,) + (0,) * (nd - 1))

    def const(shape):
        nd = len(shape)
        return pl.BlockSpec(shape, lambda bi, p, pt: (0,) * nd, pipeline_mode=pl.Buffered(1))

    hbm = pl.BlockSpec(memory_space=pl.ANY)
    ins = [q3, qrr, latn, krnt, tabs_new[0], tabs_new[1], tabs_past[0], tabs_past[1],
           w['khg'], w['grc'], w['wukp'], w['wukt'], w['wuv'], cache_ckv, cache_krt]
    in_specs = [per_b(a.shape) for a in ins[:4]] + [const(a.shape) for a in ins[4:13]] + [hbm, hbm]
    grid_spec = pltpu.PrefetchScalarGridSpec(
        num_scalar_prefetch=1,
        grid=(b, steps),
        in_specs=in_specs,
        out_specs=pl.BlockSpec((1, t_new, MLA_VW), lambda bi, p, pt: (bi, 0, 0)),
        scratch_shapes=[pltpu.VMEM((2, tt, KV_LORA), F32), pltpu.VMEM((2, MLA_ROPE, tt), F32),
                        pltpu.SemaphoreType.DMA((2, 2)),
                        pltpu.VMEM((nq + MLA_NOPEW, KV_LORA), BF16), pltpu.VMEM((nq, LANES), F32),
                        pltpu.VMEM((nq, LANES), F32), pltpu.VMEM((nq, KV_LORA), F32)],
    )
    return pl.pallas_call(
        functools.partial(_paged_kernel, layer=layer, n_pg=n_pg, t_new=t_new),
        grid_spec=grid_spec,
        out_shape=jax.ShapeDtypeStruct((b, t_new, MLA_VW), F32),
        compiler_params=_params(("arbitrary", "arbitrary")),
        name="paged_attn",
    )(page_table_flat, *ins)


def _even_back_kernel(x_ref, oa_ref, ga_ref, ob_ref, gb_ref, on_ref, wa_ref, wb_ref, y_ref):
    oa = oa_ref[...]
    on = on_ref[...]
    parts = []
    for h in range(GLA_HEADS):
        parts.append(_rms(oa[:, h * GLA_DV:(h + 1) * GLA_DV], on))
    a = jnp.concatenate(parts, axis=-1) * ga_ref[...]
    b_ = ob_ref[...] * gb_ref[...]
    y_ref[...] = x_ref[...] + _dot(a.astype(BF16), wa_ref[...]) + _dot(b_.astype(BF16), wb_ref[...])


def _even_back(x2, oa, ga, ob, gb, w, tm):
    n, d = x2.shape
    row = lambda w_: pl.BlockSpec((tm, w_), lambda i: (i, 0))
    ins = [x2, oa, ga, ob, gb, w['on'], w['wout_a'], w['wout_b']]
    return pl.pallas_call(
        _even_back_kernel,
        grid=(n // tm,),
        in_specs=[row(d), row(GLA_VW), row(GLA_VW), row(MLA_VW), row(MLA_VW)] + [_full(a.shape) for a in ins[5:]],
        out_specs=row(d),
        out_shape=jax.ShapeDtypeStruct((n, d), F32),
        compiler_params=_params(("parallel",)),
        name="even_back",
    )(*ins)


def _group_scan(a, b):
    rows, width = a.shape
    a = a.reshape(rows // SUBLANES, SUBLANES, width)
    b = b.reshape(rows // SUBLANES, SUBLANES, width)
    t = lax.broadcasted_iota(jnp.int32, a.shape, 1)
    for s in (1, 2, 4):
        keep = t >= s
        ar = pltpu.roll(a, s, 1)
        br = pltpu.roll(b, s, 1)
        b = jnp.where(keep, a * br + b, b)
        a = jnp.where(keep, a * ar, a)
    return a.reshape(rows, width), b.reshape(rows, width)


def _odd_gate_dots(xc, wax_ref):
    xcb = xc.astype(BF16)
    rs, is_ = [], []
    for n in range(RNN_BLOCKS):
        ga = _dot(xcb[:, n * RNN_BW:(n + 1) * RNN_BW], wax_ref[n])
        rs.append(ga[:, :RNN_BW])
        is_.append(ga[:, RNN_BW:])
    return jnp.concatenate(rs, axis=-1), jnp.concatenate(is_, axis=-1)


def _odd_gates(xc, wax_ref, ba_ref, bx_ref, lam_ref):
    return _odd_gate_act(xc, _odd_gate_dots(xc, wax_ref), ba_ref, bx_ref, lam_ref)


def _odd_gate_act(xc, pre, ba_ref, bx_ref, lam_ref):
    tr = jnp.tanh(pre[0] + ba_ref[...])
    ti = jnp.tanh(pre[1] + bx_ref[...])
    lam = lam_ref[...]
    sp_neg_lam = jnp.maximum(-lam, 0.0) + jnp.log1p(jnp.exp(-jnp.abs(lam)))
    c1 = (-0.5 * LRU_C) * sp_neg_lam
    a = jnp.exp(c1 * tr + c1)
    b = jnp.exp2(0.5 * jnp.log2(1.0 - a * a)) * xc * (0.5 * ti + 0.5)
    return a, b


def _odd_prompt_kernel(x_ref, ln_ref, wu_ref, wg_ref, cw_ref, cb_ref, wax_ref, ba_ref, bx_ref, lam_ref,
                       wout_ref, y_ref, hl_ref, cv_ref, ush_scr, hc_scr, *, tt):
    tb = pl.program_id(1)
    ns = SUBLANES

    @pl.when(tb == 0)
    def _():
        ush_scr[0:ns, :] = jnp.zeros((ns, D_RNN), F32)
        hc_scr[...] = jnp.zeros_like(hc_scr)

    x = x_ref[0]
    xb = _rms(x, ln_ref[...]).astype(BF16)
    u = _dot(xb, wu_ref[...])
    gate = _dot(xb, wg_ref[...])
    ush_scr[ns:ns + tt, :] = u
    cw = cw_ref[...]
    sr = tt // ODD_SUBBLOCKS
    xcs, pres = [], []
    for j in range(ODD_SUBBLOCKS):
        r0 = j * sr
        xc = cb_ref[...] + u[r0:r0 + sr] * cw[CONV_W - 1:CONV_W, :]
        for s in range(1, CONV_W):
            xc = xc + ush_scr[ns - s + r0:ns - s + r0 + sr, :] * cw[CONV_W - 1 - s:CONV_W - s, :]
        xcs.append(xc)
        pres.append(_odd_gate_dots(xc, wax_ref))
    ush_scr[0:ns, :] = ush_scr[tt:tt + ns, :]
    c = hc_scr[...]
    for j in range(ODD_SUBBLOCKS):
        r0 = j * sr
        a, b = _odd_gate_act(xcs[j], pres[j], ba_ref, bx_ref, lam_ref)
        a, b = _group_scan(a, b)
        hs = []
        for g in range(sr // ns):
            hg = a[g * ns:(g + 1) * ns] * c + b[g * ns:(g + 1) * ns]
            hs.append(hg)
            c = hg[ns - 1:ns, :]
        h = jnp.concatenate(hs, axis=0)
        y_ref[0, r0:r0 + sr, :] = x[r0:r0 + sr] + _dot((h * _half_silu(gate[r0:r0 + sr])).astype(BF16),
                                                      wout_ref[...])
    hc_scr[...] = c

    @pl.when(tb == pl.num_programs(1) - 1)
    def _():
        hl_ref[0] = c
        cv_ref[0] = ush_scr[ns - (CONV_W - 1):ns, :]


def _odd_sample_kernel(x_ref, ext_ref, ln_ref, wu_ref, wg_ref, cw_ref, cb_ref, wax_ref, ba_ref, bx_ref, lam_ref,
                       wout_ref, y_ref, h_ref, u_ref):
    ns = SUBLANES
    x = x_ref[...]
    rows = x.shape[0]
    xb = _rms(x, ln_ref[...]).astype(BF16)
    u = _dot(xb, wu_ref[...])
    gate = _dot(xb, wg_ref[...])
    u_ref[...] = u
    ext = ext_ref[...]
    t = lax.broadcasted_iota(jnp.int32, (rows, D_RNN), 0) % ns
    cw = cw_ref[...]
    xc = cb_ref[...] + u * cw[CONV_W - 1:CONV_W, :]
    for s in range(1, CONV_W):
        prev = jnp.where(t < s, pltpu.roll(ext, rows + s - ns, 0), pltpu.roll(u, s, 0))
        xc = xc + prev * cw[CONV_W - 1 - s:CONV_W - s, :]
    a, b = _odd_gates(xc, wax_ref, ba_ref, bx_ref, lam_ref)
    h0_at0 = jnp.where(t == 0, pltpu.roll(ext, rows - (ns - CONV_W), 0), 0.0)
    _, h = _group_scan(a, b + a * h0_at0)
    h_ref[...] = h
    y_ref[...] = x + _dot((h * _half_silu(gate)).astype(BF16), wout_ref[...])


_ODD_WEIGHTS = ('ln', 'wu', 'wg', 'cw', 'cb', 'wax', 'ba', 'bx', 'lam', 'wout')


def _odd_prompt(x, w, tt):
    b, t, d = x.shape
    wl = [w[n] for n in _ODD_WEIGHTS]
    return pl.pallas_call(
        functools.partial(_odd_prompt_kernel, tt=tt),
        grid=(b, t // tt),
        in_specs=[pl.BlockSpec((1, tt, d), lambda i, j: (i, j, 0))] + [_resident(a.shape) for a in wl],
        out_specs=[pl.BlockSpec((1, tt, d), lambda i, j: (i, j, 0)),
                   pl.BlockSpec((1, 1, D_RNN), lambda i, j: (i, 0, 0)),
                   pl.BlockSpec((1, CONV_W - 1, D_RNN), lambda i, j: (i, 0, 0))],
        out_shape=[jax.ShapeDtypeStruct((b, t, d), F32), jax.ShapeDtypeStruct((b, 1, D_RNN), F32),
                   jax.ShapeDtypeStruct((b, CONV_W - 1, D_RNN), F32)],
        scratch_shapes=[pltpu.VMEM((tt + SUBLANES, D_RNN), F32), pltpu.VMEM((1, D_RNN), F32)],
        compiler_params=_params(("parallel", "arbitrary")),
        name="odd_prompt",
    )(x, *wl)


def _odd_sample(x2, ext, w, rows):
    n, d = x2.shape
    wl = [w[n_] for n_ in _ODD_WEIGHTS]
    row = lambda w_: pl.BlockSpec((rows, w_), lambda i: (i, 0))
    return pl.pallas_call(
        _odd_sample_kernel,
        grid=(n // rows,),
        in_specs=[row(d), row(D_RNN)] + [_resident(a.shape) for a in wl],
        out_specs=[row(d), row(D_RNN), row(D_RNN)],
        out_shape=[jax.ShapeDtypeStruct((n, d), F32), jax.ShapeDtypeStruct((n, D_RNN), F32),
                   jax.ShapeDtypeStruct((n, D_RNN), F32)],
        compiler_params=_params(("parallel",)),
        name="odd_sample",
    )(x2, ext, *wl)


def _pad_heads(v, n_used):
    lead = v.shape[:-1]
    v = v.reshape(lead + (MLA_HEADS, n_used))
    v = jnp.pad(v, [(0, 0)] * len(lead) + [(0, 0), (0, HEAD_PAD - n_used)])
    return v.reshape(lead + (MLA_PADW,))


def _rot_half(w):
    return jnp.concatenate([-w[..., HALF_ROPE:], w[..., :HALF_ROPE]], axis=-1)


def _swap_half_gain(g):
    return jnp.concatenate([g[:MLA_NOPE], g[MLA_NOPE + HALF_ROPE:], g[MLA_NOPE:MLA_NOPE + HALF_ROPE]])


def _pack_even(j, ln_even, w_in_even, gla_w_f2, gla_b_f, gla_out_norm, mla_q_norm, mla_kv_norm, mla_w_uq,
               mla_w_ukv, mla_qh_norm, mla_kh_norm, w_out_even):
    wi = w_in_even[j]
    d = wi.shape[0]
    o = 0
    seg = {}
    for name, width in (('qa', GLA_QK), ('ka', GLA_QK), ('va', GLA_VW), ('ga', GLA_VW), ('fa', GLA_RANK),
                        ('cq', Q_LORA), ('ckv', KV_LORA), ('kr', MLA_ROPE), ('gb', MLA_VW)):
        seg[name] = wi[:, o:o + width]
        o += width
    z = lambda n: jnp.zeros((d, n), wi.dtype)
    tail = MISC_W - ROPE_LANE0 - MLA_ROPE
    misc = jnp.concatenate([seg['fa'], z(ROPE_LANE0 - GLA_RANK), seg['kr'], z(tail)], 1)
    misc_rot = jnp.concatenate([z(ROPE_LANE0), _rot_half(seg['kr']), z(tail)], 1)
    win = jnp.concatenate([seg['qa'], seg['ka'], seg['va'], 0.5 * seg['ga'], seg['cq'], seg['ckv'],
                           0.5 * seg['gb'], misc, misc_rot], 1)
    wf2 = jnp.pad(gla_w_f2[j], ((0, MISC_W - GLA_RANK), (0, 0)))
    ukv = mla_w_ukv[j].reshape(KV_LORA, MLA_HEADS, MLA_NOPE + MLA_V)
    wuk = ukv[:, :, :MLA_NOPE].reshape(KV_LORA, MLA_NOPEW)
    wuv = ukv[:, :, MLA_NOPE:].reshape(KV_LORA, MLA_VW)
    uq = mla_w_uq[j].reshape(Q_LORA, MLA_HEADS, MLA_QK)
    uq_rot = jnp.concatenate([jnp.zeros_like(uq[..., :MLA_NOPE]), _rot_half(uq[..., MLA_NOPE:])], -1)
    khp = jnp.pad(mla_kh_norm[j], (0, HEAD_PAD - MLA_QK))
    return {
        'ln': ln_even[j][None, :],
        'win': win.astype(BF16),
        'wf2': wf2.astype(BF16),
        'bf': gla_b_f[j][None, :],
        'qn': mla_q_norm[j][None, :],
        'kvn': mla_kv_norm[j][None, :],
        'wuq': _pad_heads(mla_w_uq[j], MLA_QK).astype(BF16),
        'wuqr': _pad_heads(uq_rot.reshape(Q_LORA, MLA_HEADS * MLA_QK), MLA_QK).astype(BF16),
        'wukp': _pad_heads(wuk, MLA_NOPE).astype(BF16),
        'wukt': wuk.T.astype(BF16),
        'wuv': wuv.astype(BF16),
        'wuvt': wuv.T.astype(BF16),
        'qh': jnp.pad(mla_qh_norm[j], (0, HEAD_PAD - MLA_QK))[None, :],
        'qhs': jnp.pad(_swap_half_gain(mla_qh_norm[j]), (0, HEAD_PAD - MLA_QK))[None, :],
        'kh': khp[None, :],
        'khs': jnp.pad(_swap_half_gain(mla_kh_norm[j]), (0, HEAD_PAD - MLA_QK))[None, :],
        'khg': jnp.tile(khp, MLA_HEADS)[None, :],
        'grc': mla_kh_norm[j][MLA_NOPE:][:, None],
        'on': gla_out_norm[j][None, :],
        'wout_a': w_out_even[j][:GLA_VW].astype(BF16),
        'wout_b': w_out_even[j][GLA_VW:].astype(BF16),
    }


def _pack_odd(j, ln_odd, w_in_odd, conv_w, conv_b, rg_w_a, rg_b_a, rg_w_x, rg_b_x, rg_lambda, w_out_odd):
    return {
        'ln': ln_odd[j][None, :],
        'wu': w_in_odd[j][:, :D_RNN].astype(BF16),
        'wg': (0.5 * w_in_odd[j][:, D_RNN:]).astype(BF16),
        'cw': conv_w[j],
        'cb': conv_b[j][None, :],
        'wax': (0.5 * jnp.concatenate([rg_w_a[j], rg_w_x[j]], axis=-1)).astype(BF16),
        'ba': 0.5 * rg_b_a[j][None, :],
        'bx': 0.5 * rg_b_x[j][None, :],
        'lam': rg_lambda[j][None, :],
        'wout': w_out_odd[j].astype(BF16),
    }


def _rope_angles(pos):
    inv_freq = ROPE_THETA ** (-jnp.arange(HALF_ROPE, dtype=F32) / HALF_ROPE)
    ang = pos.astype(F32)[:, None] * inv_freq[None, :]
    return jnp.cos(ang), jnp.sin(ang)


def _rope_tile_tables(pos):
    cos, sin = _rope_angles(pos)
    n = pos.shape[0]
    tail = LANES - ROPE_LANE0 - MLA_ROPE
    c = jnp.concatenate([jnp.ones((n, ROPE_LANE0), F32), cos, cos, jnp.ones((n, tail), F32)], 1)
    s = jnp.concatenate([jnp.zeros((n, ROPE_LANE0), F32), sin, sin, jnp.zeros((n, tail), F32)], 1)
    return c, s


def _rope_pair_tables_t(pos):
    cos, sin = _rope_angles(pos)
    return jnp.concatenate([cos, cos], 1).T, jnp.concatenate([sin, sin], 1).T


def _pick(n, prefs):
    for p in prefs:
        if n % p == 0:
            return p
    return n


def kernel(x_prompt, x_sample, state_gla, cache_ckv, cache_kr, state_rglru_h, state_rglru_conv, page_table,
           ln_even, w_in_even, gla_w_f2, gla_b_f, gla_out_norm, mla_q_norm, mla_kv_norm, mla_w_uq, mla_w_ukv,
           mla_qh_norm, mla_kh_norm, w_out_even, ln_odd, w_in_odd, conv_w, conv_b, rg_w_a, rg_b_a, rg_w_x,
           rg_b_x, rg_lambda, w_out_odd):
    bp, tp, d = x_prompt.shape
    bs, ts, _ = x_sample.shape
    n_pages = page_table.shape[1]
    past_len = n_pages * PAGE_SIZE
    depth = ln_even.shape[0] + ln_odd.shape[0]
    assert ts == SUBLANES, "sample group is handled as one 8-row group per sequence"
    assert MLA_NOPE == MLA_V

    tm_p = _pick(tp, (256, 128, 64, 32, 16, 8))
    tm_s = _pick(bs * ts, (256, 128, 64, 32, 16, 8))
    tm_back = _pick(tp, (512, 256, 128, 64, 32, 16, 8))
    nseq_p = _pick(bp, (2, 1))
    nseq_s = _pick(bs, (8, 4, 2, 1))
    chunk_p = min(GLA_CHUNK, tp)
    tblk_p = _pick(tp, (256, 128, 64)) if tp >= GLA_CHUNK else tp
    tq = _pick(tp, (512, 256, 128))
    tt_odd = _pick(tp, (256, 128, 64))
    n_pg = _pick(n_pages, (PAGED_KEYS_PER_STEP // PAGE_SIZE, 32, 16, 8, 4, 2, 1))

    tabs_p = _rope_tile_tables(jnp.arange(tp))
    pos_s = past_len + jnp.arange(ts)
    tabs_s = tuple(jnp.tile(a, (tm_s // ts, 1)) for a in _rope_tile_tables(pos_s))
    pair_new = _rope_pair_tables_t(pos_s)
    pair_past = _rope_pair_tables_t(jnp.arange(past_len))
    pt_flat = page_table.reshape(-1).astype(jnp.int32)
    cache_krt = jnp.swapaxes(cache_kr, -1, -2)

    yp = x_prompt.reshape(bp * tp, d)
    ys = x_sample.reshape(bs * ts, d)
    outs = {k: [] for k in ('gla_p', 'ckv_p', 'kr_p', 'rh_p', 'rc_p', 'gla_s', 'ckv_s', 'kr_s', 'rh_s', 'rc_s')}
    for layer in range(depth):
        j = layer // 2
        if layer % 2 == 0:
            w = _pack_even(j, ln_even, w_in_even, gla_w_f2, gla_b_f, gla_out_norm, mla_q_norm, mla_kv_norm,
                           mla_w_uq, mla_w_ukv, mla_qh_norm, mla_kh_norm, w_out_even)
            qa, ka, va, lf, ga, gb, q, k, v, ckvn, misc = _even_front(yp, w, tabs_p, tm_p, tp)
            r3 = lambda a: a.reshape(bp, tp, a.shape[-1])
            oa, st = _gla_scan(r3(qa), r3(ka), r3(va), r3(lf), None, chunk_p, tblk_p, nseq_p)
            ob = _flash_attn(r3(q), r3(k), v, tq, FLASH_HEADS_PER_STEP)
            yp = _even_back(yp, oa.reshape(bp * tp, GLA_VW), ga, ob.reshape(bp * tp, MLA_VW), gb, w, tm_back)
            outs['gla_p'].append(st.reshape(bp, GLA_HEADS, GLA_DK, GLA_DV))
            outs['ckv_p'].append(ckvn.reshape(bp, tp, KV_LORA))
            outs['kr_p'].append(misc[:, ROPE_LANE0:ROPE_LANE0 + MLA_ROPE].reshape(bp, tp, MLA_ROPE))
            qa, ka, va, lf, ga, gb, q, k, v, ckvn, misc = _even_front(ys, w, tabs_s, tm_s, ts)
            r3 = lambda a: a.reshape(bs, ts, a.shape[-1])
            s0t = state_gla[j].reshape(bs, GLA_HEADS // 2, 2 * GLA_DK, GLA_DV)
            oa, st = _gla_scan(r3(qa), r3(ka), r3(va), r3(lf), s0t, ts, ts, nseq_s)
            krs = misc[:, ROPE_LANE0:ROPE_LANE0 + MLA_ROPE].reshape(bs, ts, MLA_ROPE)
            q4 = q.reshape(bs, ts, MLA_HEADS, HEAD_PAD)
            qrope = jnp.swapaxes(q4[..., ROPE_LANE0:ROPE_LANE0 + MLA_ROPE], 1, 2)
            qrope = qrope.reshape(bs, MLA_HEADS * ts, MLA_ROPE)
            qrr = jnp.concatenate([qrope, qrope[..., HALF_ROPE:], -qrope[..., :HALF_ROPE]], axis=-1)
            ob = _paged_attn(j, pt_flat, n_pages, r3(q), qrr, r3(ckvn), jnp.swapaxes(krs, 1, 2), pair_new,
                             pair_past, w, cache_ckv, cache_krt, n_pg)
            ys = _even_back(ys, oa.reshape(bs * ts, GLA_VW), ga, ob.reshape(bs * ts, MLA_VW), gb, w, tm_s)
            outs['gla_s'].append(st.reshape(bs, GLA_HEADS, GLA_DK, GLA_DV))
            outs['ckv_s'].append(ckvn.reshape(bs, ts, KV_LORA))
            outs['kr_s'].append(krs)
        else:
            w = _pack_odd(j, ln_odd, w_in_odd, conv_w, conv_b, rg_w_a, rg_b_a, rg_w_x, rg_b_x, rg_lambda,
                          w_out_odd)
            y3, hl, cv = _odd_prompt(yp.reshape(bp, tp, d), w, tt_odd)
            yp = y3.reshape(bp * tp, d)
            outs['rh_p'].append(hl.reshape(bp, D_RNN))
            outs['rc_p'].append(cv)
            ext = jnp.concatenate([jnp.zeros((bs, ts - CONV_W, D_RNN), F32), state_rglru_h[j][:, None, :],
                                   state_rglru_conv[j]], axis=1).reshape(bs * ts, D_RNN)
            ys, hfull, ufull = _odd_sample(ys, ext, w, tm_s)
            outs['rh_s'].append(hfull.reshape(bs, ts, D_RNN)[:, ts - 1])
            outs['rc_s'].append(ufull.reshape(bs, ts, D_RNN)[:, ts - (CONV_W - 1):])
    st_ = lambda name: jnp.stack(outs[name])
    return (yp.reshape(bp, tp, d), ys.reshape(bs, ts, d),
            st_('gla_p'), st_('ckv_p'), st_('kr_p'), st_('rh_p'), st_('rc_p'),
            st_('gla_s'), st_('ckv_s'), st_('kr_s'), st_('rh_s'), st_('rc_s'))
```

```python
import functools

import jax
import jax.numpy as jnp
from jax import lax
from jax.experimental import pallas as pl
from jax.experimental.pallas import tpu as pltpu

F32 = jnp.float32
BF16 = jnp.bfloat16
EPS = 1e-6

GLA_HEADS = 4
GLA_DK = 64
GLA_DV = 128
GLA_RANK = 16
GLA_TAU = 16.0
GLA_CHUNK = 64
GLA_QK = GLA_HEADS * GLA_DK
GLA_VW = GLA_HEADS * GLA_DV
MLA_HEADS = 8
MLA_NOPE = 64
MLA_ROPE = 32
MLA_V = 64
MLA_QK = MLA_NOPE + MLA_ROPE
MLA_VW = MLA_HEADS * MLA_V
MLA_NOPEW = MLA_HEADS * MLA_NOPE
Q_LORA = 384
KV_LORA = 256
ROPE_THETA = 10000.0
PAGE_SIZE = 128
D_RNN = 1280
RNN_BLOCKS = 10
RNN_BW = D_RNN // RNN_BLOCKS
CONV_W = 4
LRU_C = 8.0

LANES = 128
SUBLANES = 8
HEAD_PAD = LANES
MLA_PADW = MLA_HEADS * HEAD_PAD
HALF_ROPE = MLA_ROPE // 2
ROPE_LANE0 = MLA_NOPE
MISC_W = LANES
LOG2E = 1.4426950408889634
Q_SCALE = MLA_QK ** -0.5 * LOG2E
FLASH_HEADS_PER_STEP = 8
FLASH_TK = 512
EXP_CLAMP = 80.0
NEG_BIG = -1e30
VMEM_LIMIT = 56 * 1024 * 1024
PAGED_KEYS_PER_STEP = 8192

NT_DIMS = (((1,), (1,)), ((), ()))
TN_DIMS = (((0,), (0,)), ((), ()))


def _dot(a, b):
    return jnp.dot(a, b, preferred_element_type=F32)


def _dot_nt(a, b):
    return lax.dot_general(a, b, NT_DIMS, preferred_element_type=F32)


def _dot_tn(a, b):
    return lax.dot_general(a, b, TN_DIMS, preferred_element_type=F32)


def _rms(x, g, n=None):
    n = x.shape[-1] if n is None else n
    ss = jnp.sum(x * x, axis=-1, keepdims=True) * (1.0 / n)
    return x * lax.rsqrt(ss + EPS) * g


def _half_silu(gh):
    return gh * jnp.tanh(gh) + gh


def _log_sigmoid(x):
    return jnp.minimum(x, 0.0) - jnp.log1p(jnp.exp(-jnp.abs(x)))


def _params(sem):
    return pltpu.CompilerParams(dimension_semantics=sem, vmem_limit_bytes=VMEM_LIMIT)


def _full(shape):
    nd = len(shape)
    return pl.BlockSpec(shape, lambda *_: (0,) * nd)


def _resident(shape):
    nd = len(shape)
    return pl.BlockSpec(shape, lambda *_: (0,) * nd, pipeline_mode=pl.Buffered(1))


def _even_front_kernel(x_ref, ln_ref, win_ref, wf2_ref, bf_ref, qn_ref, kvn_ref, wuq_ref, wuqr_ref, wuk_ref,
                       wuv_ref, qh_ref, qhs_ref, kh_ref, khs_ref, c_ref, s_ref,
                       qa_ref, ka_ref, va_ref, lf_ref, ga_ref, gb_ref, q_ref, k_ref, v_ref,
                       ckv_ref, misc_ref):
    x = x_ref[...]
    xn = _rms(x, ln_ref[...])
    z = _dot(xn.astype(BF16), win_ref[...])
    o = 0
    qa_ref[...] = z[:, o:o + GLA_QK] * (GLA_DK ** -0.5); o += GLA_QK
    ka_ref[...] = z[:, o:o + GLA_QK]; o += GLA_QK
    va_ref[...] = z[:, o:o + GLA_VW]; o += GLA_VW
    ga_ref[...] = _half_silu(z[:, o:o + GLA_VW]); o += GLA_VW
    cq = z[:, o:o + Q_LORA]; o += Q_LORA
    ckv = z[:, o:o + KV_LORA]; o += KV_LORA
    gb_ref[...] = _half_silu(z[:, o:o + MLA_VW]); o += MLA_VW
    misc = z[:, o:o + MISC_W]; o += MISC_W
    misc_rot = z[:, o:o + MISC_W]
    misc_ref[...] = misc

    f = _dot(misc.astype(BF16), wf2_ref[...]) + bf_ref[...]
    lf_ref[...] = _log_sigmoid(f) * (1.0 / GLA_TAU)

    c = c_ref[...]
    s = s_ref[...]

    cqb = _rms(cq, qn_ref[...]).astype(BF16)
    ckvn = _rms(ckv, kvn_ref[...])
    ckv_ref[...] = ckvn
    cb = ckvn.astype(BF16)
    qraw = _dot(cqb, wuq_ref[...])
    qrot = _dot(cqb, wuqr_ref[...])
    kraw = _dot(cb, wuk_ref[...])
    v_ref[...] = _dot_nt(wuv_ref[...], cb).astype(BF16)
    gc = qh_ref[...] * c * Q_SCALE
    gs = qhs_ref[...] * s * Q_SCALE
    for h in range(MLA_HEADS):
        hs = slice(h * HEAD_PAD, (h + 1) * HEAD_PAD)
        t = qraw[:, hs]
        r = lax.rsqrt(jnp.sum(t * t, axis=-1, keepdims=True) * (1.0 / MLA_QK) + EPS)
        q_ref[:, hs] = ((t * gc + qrot[:, hs] * gs) * r).astype(BF16)

    lane = lax.broadcasted_iota(jnp.int32, misc.shape, 1)
    krt = jnp.where(lane >= ROPE_LANE0, misc, 0.0)
    gc = kh_ref[...] * c
    rot_gs = misc_rot * (khs_ref[...] * s)
    for h in range(MLA_HEADS):
        hs = slice(h * HEAD_PAD, (h + 1) * HEAD_PAD)
        t = kraw[:, hs] + krt
        r = lax.rsqrt(jnp.sum(t * t, axis=-1, keepdims=True) * (1.0 / MLA_QK) + EPS)
        k_ref[:, hs] = ((t * gc + rot_gs) * r).astype(BF16)


def _even_front(x2, w, tabs, tm, rows_per_seq):
    n, d = x2.shape
    nblk_per_seq = rows_per_seq // tm if rows_per_seq >= tm else None
    row = lambda w_: pl.BlockSpec((tm, w_), lambda i: (i, 0))
    if nblk_per_seq is not None:
        tab = pl.BlockSpec((tm, LANES), lambda i: (i % nblk_per_seq, 0))
    else:
        tab = pl.BlockSpec((tm, LANES), lambda i: (0, 0))
    ins = [x2, w['ln'], w['win'], w['wf2'], w['bf'], w['qn'], w['kvn'], w['wuq'], w['wuqr'], w['wukp'],
           w['wuvt'], w['qh'], w['qhs'], w['kh'], w['khs'], tabs[0], tabs[1]]
    in_specs = [row(d)] + [_resident(a.shape) for a in ins[1:15]] + [tab, tab]
    outs = [(GLA_QK, F32), (GLA_QK, F32), (GLA_VW, F32), (GLA_QK, F32), (GLA_VW, F32), (MLA_VW, F32),
            (MLA_PADW, BF16), (MLA_PADW, BF16), None, (KV_LORA, F32), (MISC_W, F32)]
    vt_spec = pl.BlockSpec((MLA_VW, tm), lambda i: (0, i))
    vt_shape = jax.ShapeDtypeStruct((MLA_VW, n), BF16)
    return pl.pallas_call(
        _even_front_kernel,
        grid=(n // tm,),
        in_specs=in_specs,
        out_specs=[vt_spec if o_ is None else row(o_[0]) for o_ in outs],
        out_shape=[vt_shape if o_ is None else jax.ShapeDtypeStruct((n, o_[0]), o_[1]) for o_ in outs],
        compiler_params=_params(("parallel",)),
        name="even_front",
    )(*ins)


def _gla_kernel(*refs, chunk, n_chunks, n_seq, has_s0):
    if has_s0:
        q_ref, k_ref, v_ref, g_ref, s0_ref, o_ref, st_ref, s_scr = refs
    else:
        q_ref, k_ref, v_ref, g_ref, o_ref, st_ref, s_scr = refs
    tb = pl.program_id(1)

    @pl.when(tb == 0)
    def _():
        if has_s0:
            s_scr[...] = s0_ref[...]
        else:
            s_scr[...] = jnp.zeros_like(s_scr)

    c_ = chunk
    row = lax.broadcasted_iota(jnp.int32, (c_, c_), 0)
    col = lax.broadcasted_iota(jnp.int32, (c_, c_), 1)
    tri = row >= col
    tri_bf = jnp.where(tri, 1.0, 0.0).astype(BF16)
    mid_row = c_ // 2 - 1
    lane_head = lax.broadcasted_iota(jnp.int32, (c_, LANES), 1) // GLA_DK
    zero_bf = jnp.zeros((c_, LANES), BF16)
    for c in range(n_chunks):
        sl = slice(c * c_, (c + 1) * c_)
        cums = []
        for b_ in range(n_seq):
            g = g_ref[b_, sl, :]
            g_hi = g.astype(BF16)
            g_lo = (g - g_hi.astype(F32)).astype(BF16)
            cums.append(_dot(tri_bf, g_hi) + _dot(tri_bf, g_lo))
        work = []
        for b_ in range(n_seq):
            cum = cums[b_]
            tot = cum[c_ - 1:c_, :]
            mid = cum[mid_row:mid_row + 1, :]
            q = q_ref[b_, sl, :]
            k = k_ref[b_, sl, :]
            qd = (q * jnp.exp(cum)).astype(BF16)
            qm = (q * jnp.exp(jnp.minimum(cum - mid, EXP_CLAMP))).astype(BF16)
            km = (k * jnp.exp(jnp.minimum(mid - cum, EXP_CLAMP))).astype(BF16)
            kd = (k * jnp.exp(tot - cum)).astype(BF16)
            etot = jnp.exp(tot)
            for pr in range(GLA_HEADS // 2):
                ts = slice(pr * LANES, (pr + 1) * LANES)
                heads = []
                for hh in range(2):
                    h = 2 * pr + hh
                    mine = lane_head == hh
                    vs = slice(h * GLA_DV, (h + 1) * GLA_DV)
                    vh = v_ref[b_, sl, vs].astype(BF16)
                    a = _dot_nt(jnp.where(mine, qm[:, ts], zero_bf), km[:, ts])
                    heads.append((vs, vh, a, jnp.where(mine, qd[:, ts], zero_bf),
                                  jnp.where(mine, kd[:, ts], zero_bf)))
                work.append((b_, pr, etot[:, ts], heads))
        upds = [_dot_tn(jnp.concatenate([hd[4] for hd in heads], axis=0),
                        jnp.concatenate([hd[1] for hd in heads], axis=0)) for _, _, _, heads in work]
        for (b_, pr, etot_t, heads), upd in zip(work, upds):
            st = s_scr[b_, pr]
            st_bf = st.astype(BF16)
            for vs, vh, a, qdm, _ in heads:
                a = jnp.where(tri, a, 0.0).astype(BF16)
                if c_ % GLA_DK == 0:
                    o_ref[b_, sl, vs] = _dot(jnp.concatenate([qdm, a], axis=1),
                                             jnp.concatenate([st_bf, vh], axis=0))
                else:
                    o_ref[b_, sl, vs] = _dot(qdm, st_bf) + _dot(a, vh)
            decay = jnp.transpose(jnp.broadcast_to(etot_t, (LANES, LANES)))
            s_scr[b_, pr] = st * decay + upd

    @pl.when(tb == pl.num_programs(1) - 1)
    def _():
        st_ref[...] = s_scr[...]


def _gla_scan(qa, ka, va, lf, s0t, chunk, tblk, n_seq):
    b, t, _ = qa.shape
    n_chunks = tblk // chunk
    has_s0 = s0t is not None
    seq = lambda w_: pl.BlockSpec((n_seq, tblk, w_), lambda i, j: (i, j, 0))
    st_spec = pl.BlockSpec((n_seq, GLA_HEADS // 2, 2 * GLA_DK, GLA_DV), lambda i, j: (i, 0, 0, 0))
    ins = [qa, ka, va, lf] + ([s0t] if has_s0 else [])
    in_specs = [seq(GLA_QK), seq(GLA_QK), seq(GLA_VW), seq(GLA_QK)] + ([st_spec] if has_s0 else [])
    return pl.pallas_call(
        functools.partial(_gla_kernel, chunk=chunk, n_chunks=n_chunks, n_seq=n_seq, has_s0=has_s0),
        grid=(b // n_seq, t // tblk),
        in_specs=in_specs,
        out_specs=[seq(GLA_VW), st_spec],
        out_shape=[jax.ShapeDtypeStruct((b, t, GLA_VW), F32),
                   jax.ShapeDtypeStruct((b, GLA_HEADS // 2, 2 * GLA_DK, GLA_DV), F32)],
        scratch_shapes=[pltpu.VMEM((n_seq, GLA_HEADS // 2, 2 * GLA_DK, GLA_DV), F32)],
        compiler_params=_params(("parallel", "arbitrary")),
        name="gla_scan",
    )(*ins)


def _flash_kernel(q_ref, k_ref, vt_ref, o_ref, m_scr, l_scr, acc_scr, *, tq, tk, n_heads):
    qi = pl.program_id(2)
    m_scr[...] = jnp.full_like(m_scr, NEG_BIG)
    l_scr[...] = jnp.zeros_like(l_scr)
    acc_scr[...] = jnp.zeros_like(acc_scr)

    def block(ki, diagonal):
        ks = pl.ds(pl.multiple_of(ki * tk, tk), tk)
        sts = []
        for h in range(n_heads):
            kh = k_ref[0, ks, h * HEAD_PAD:(h + 1) * HEAD_PAD]
            qh = q_ref[0, :, h * HEAD_PAD:(h + 1) * HEAD_PAD]
            sts.append(_dot_nt(kh, qh))
        for h in range(n_heads):
            st = sts[h]
            if diagonal:
                key = lax.broadcasted_iota(jnp.int32, (tk, tq), 0) + ki * tk
                qry = lax.broadcasted_iota(jnp.int32, (tk, tq), 1) + qi * tq
                st = jnp.where(key <= qry, st, NEG_BIG)
            m_prev = m_scr[h]
            m_new = jnp.maximum(m_prev, jnp.max(st, axis=0, keepdims=True))
            alpha = jnp.exp2(m_prev - m_new)
            p = jnp.exp2(st - m_new)
            l_scr[h] = alpha * l_scr[h] + jnp.sum(p, axis=0, keepdims=True)
            m_scr[h] = m_new
            vt = vt_ref[h * MLA_V:(h + 1) * MLA_V, ks]
            acc_scr[h] = alpha * acc_scr[h] + _dot(vt, p.astype(BF16))

    def body(ki, carry):
        block(ki, False)
        return carry

    per_q = tq // tk
    lax.fori_loop(0, qi * per_q, body, 0)
    for j in range(per_q):
        block(qi * per_q + j, True)
    for pr in range(n_heads // 2):
        ot = jnp.concatenate([acc_scr[2 * pr] / l_scr[2 * pr], acc_scr[2 * pr + 1] / l_scr[2 * pr + 1]], axis=0)
        o_ref[0, :, pr * LANES:(pr + 1) * LANES] = jnp.transpose(ot)


def _flash_attn(q, k, vt, tq, n_heads):
    b, t, _ = q.shape
    nq = t // tq
    ng = MLA_HEADS // n_heads
    return pl.pallas_call(
        functools.partial(_flash_kernel, tq=tq, tk=min(FLASH_TK, tq), n_heads=n_heads),
        grid=(b, ng, nq),
        in_specs=[pl.BlockSpec((1, tq, n_heads * HEAD_PAD), lambda bi, g, i: (bi, i, g)),
                  pl.BlockSpec((1, t, n_heads * HEAD_PAD), lambda bi, g, i: (bi, 0, g)),
                  pl.BlockSpec((n_heads * MLA_V, t), lambda bi, g, i: (g, bi))],
        out_specs=pl.BlockSpec((1, tq, n_heads * MLA_V), lambda bi, g, i: (bi, i, g)),
        out_shape=jax.ShapeDtypeStruct((b, t, MLA_VW), F32),
        scratch_shapes=[pltpu.VMEM((n_heads, 1, tq), F32), pltpu.VMEM((n_heads, 1, tq), F32),
                        pltpu.VMEM((n_heads, MLA_V, tq), F32)],
        compiler_params=_params(("parallel", "parallel", "arbitrary")),
        name="flash_attn",
    )(q, k, vt)


def _paged_kernel(pt_ref, q_ref, qrr_ref, latn_ref, krnt_ref, cosn_ref, sinn_ref, cos_ref, sin_ref,
                  khg_ref, grc_ref, wukp_ref, wukt_ref, wuv_ref, ckv_hbm, krt_hbm,
                  o_ref, lat_buf, krt_buf, sem, lhs_scr, m_scr, l_scr, acc_scr, *, layer, n_pg, t_new):
    p_idx = pl.program_id(1)
    steps = pl.num_programs(1)
    step = pl.program_id(0) * steps + p_idx
    total = pl.num_programs(0) * steps
    slot = lax.rem(step, 2)
    nq = MLA_HEADS * t_new
    tt = n_pg * PAGE_SIZE

    def page_copies(st, sl):
        cps = []
        for g in range(n_pg):
            page = pt_ref[st * n_pg + g]
            rows = pl.ds(g * PAGE_SIZE, PAGE_SIZE)
            cps.append(pltpu.make_async_copy(ckv_hbm.at[layer, page], lat_buf.at[sl, rows, :], sem.at[sl, 0]))
            cps.append(pltpu.make_async_copy(krt_hbm.at[layer, page], krt_buf.at[sl, :, rows], sem.at[sl, 1]))
        return cps

    @pl.when(step == 0)
    def _():
        for cp in page_copies(0, 0):
            cp.start()

    @pl.when(step + 1 < total)
    def _():
        for cp in page_copies(step + 1, 1 - slot):
            cp.start()

    for cp in page_copies(step, slot):
        cp.wait()

    @pl.when(p_idx == 0)
    def _():
        qt = q_ref[0].astype(F32)
        qrep = jnp.concatenate([qt] * MLA_HEADS, axis=0)
        r_ = lax.broadcasted_iota(jnp.int32, qrep.shape, 0) // t_new
        c_ = lax.broadcasted_iota(jnp.int32, qrep.shape, 1) // HEAD_PAD
        qbd = jnp.where(r_ == c_, qrep * khg_ref[...], 0.0)
        lhs_scr[0:nq, :] = _dot_nt(qbd.astype(BF16), wukp_ref[...]).astype(BF16)
        lhs_scr[nq:, :] = wukt_ref[...]
        m_scr[...] = jnp.full_like(m_scr, NEG_BIG)
        l_scr[...] = jnp.zeros_like(l_scr)
        acc_scr[...] = jnp.zeros_like(acc_scr)

    grc = grc_ref[...]
    qrr = qrr_ref[0]

    def attend(cb, kt, cost, sint, mask):
        n = cb.shape[0]
        x = kt * grc
        xx = jnp.concatenate([x * cost, x * sint], axis=0).astype(BF16)
        big = _dot_nt(lhs_scr[...], cb)
        rope = _dot(qrr, xx)
        k2 = jnp.sum(kt * kt, axis=0, keepdims=True)
        rows = []
        for h in range(MLA_HEADS):
            blk = big[nq + h * MLA_NOPE:nq + (h + 1) * MLA_NOPE]
            ssq = jnp.sum(blk * blk, axis=0, keepdims=True) + k2
            rows.append(jnp.broadcast_to(lax.rsqrt(ssq * (1.0 / MLA_QK) + EPS), (t_new, n)))
        s = (big[0:nq] + rope) * jnp.concatenate(rows, axis=0)
        if mask is not None:
            s = jnp.where(mask, s, NEG_BIG)
        m_prev = m_scr[...]
        m_new = jnp.maximum(m_prev, jnp.max(s, axis=-1, keepdims=True))
        alpha = jnp.exp2(m_prev - m_new)
        p = jnp.exp2(s - m_new[:, 0:1])
        l_scr[...] = alpha * l_scr[...] + jnp.sum(p, axis=-1, keepdims=True)
        m_scr[...] = m_new
        acc_scr[...] = alpha[:, 0:1] * acc_scr[...] + _dot(p.astype(BF16), cb)

    ks = pl.ds(pl.multiple_of(p_idx * tt, tt), tt)
    attend(lat_buf[slot].astype(BF16), krt_buf[slot], cos_ref[:, ks], sin_ref[:, ks], None)

    @pl.when(p_idx == pl.num_programs(1) - 1)
    def _():
        qi = lax.broadcasted_iota(jnp.int32, (nq, t_new), 0) % t_new
        kj = lax.broadcasted_iota(jnp.int32, (nq, t_new), 1)
        attend(latn_ref[0].astype(BF16), krnt_ref[0], cosn_ref[...], sinn_ref[...], kj <= qi)
        out_lat = acc_scr[...] / l_scr[:, 0:1]
        er = lax.broadcasted_iota(jnp.int32, (nq, MLA_VW), 0) // t_new
        ec = lax.broadcasted_iota(jnp.int32, (nq, MLA_VW), 1) // MLA_V
        full = jnp.where(er == ec, _dot(out_lat.astype(BF16), wuv_ref[...]), 0.0)
        o = full[0:t_new]
        for h in range(1, MLA_HEADS):
            o = o + full[h * t_new:(h + 1) * t_new]
        o_ref[0] = o


def _paged_attn(layer, page_table_flat, n_pages, q3, qrr, latn, krnt, tabs_new, tabs_past, w,
                cache_ckv, cache_krt, n_pg):
    b, t_new, _ = q3.shape
    nq = MLA_HEADS * t_new
    steps = n_pages // n_pg
    tt = n_pg * PAGE_SIZE

    def per_b(shape):
        nd = len(shape)
        return pl.BlockSpec((1,) + shape[1:], lambda bi, p, pt: (bi,) + (0,) * (nd - 1))

    def const(shape):
        nd = len(shape)
        return pl.BlockSpec(shape, lambda bi, p, pt: (0,) * nd, pipeline_mode=pl.Buffered(1))

    hbm = pl.BlockSpec(memory_space=pl.ANY)
    ins = [q3, qrr, latn, krnt, tabs_new[0], tabs_new[1], tabs_past[0], tabs_past[1],
           w['khg'], w['grc'], w['wukp'], w['wukt'], w['wuv'], cache_ckv, cache_krt]
    in_specs = [per_b(a.shape) for a in ins[:4]] + [const(a.shape) for a in ins[4:13]] + [hbm, hbm]
    grid_spec = pltpu.PrefetchScalarGridSpec(
        num_scalar_prefetch=1,
        grid=(b, steps),
        in_specs=in_specs,
        out_specs=pl.BlockSpec((1, t_new, MLA_VW), lambda bi, p, pt: (bi, 0, 0)),
        scratch_shapes=[pltpu.VMEM((2, tt, KV_LORA), F32), pltpu.VMEM((2, MLA_ROPE, tt), F32),
                        pltpu.SemaphoreType.DMA((2, 2)),
                        pltpu.VMEM((nq + MLA_NOPEW, KV_LORA), BF16), pltpu.VMEM((nq, LANES), F32),
                        pltpu.VMEM((nq, LANES), F32), pltpu.VMEM((nq, KV_LORA), F32)],
    )
    return pl.pallas_call(
        functools.partial(_paged_kernel, layer=layer, n_pg=n_pg, t_new=t_new),
        grid_spec=grid_spec,
        out_shape=jax.ShapeDtypeStruct((b, t_new, MLA_VW), F32),
        compiler_params=_params(("arbitrary", "arbitrary")),
        name="paged_attn",
    )(page_table_flat, *ins)


def _even_back_kernel(x_ref, oa_ref, ga_ref, ob_ref, gb_ref, on_ref, wa_ref, wb_ref, y_ref):
    oa = oa_ref[...]
    on = on_ref[...]
    parts = []
    for h in range(GLA_HEADS):
        parts.append(_rms(oa[:, h * GLA_DV:(h + 1) * GLA_DV], on))
    a = jnp.concatenate(parts, axis=-1) * ga_ref[...]
    b_ = ob_ref[...] * gb_ref[...]
    y_ref[...] = x_ref[...] + _dot(a.astype(BF16), wa_ref[...]) + _dot(b_.astype(BF16), wb_ref[...])


def _even_back(x2, oa, ga, ob, gb, w, tm):
    n, d = x2.shape
    row = lambda w_: pl.BlockSpec((tm, w_), lambda i: (i, 0))
    ins = [x2, oa, ga, ob, gb, w['on'], w['wout_a'], w['wout_b']]
    return pl.pallas_call(
        _even_back_kernel,
        grid=(n // tm,),
        in_specs=[row(d), row(GLA_VW), row(GLA_VW), row(MLA_VW), row(MLA_VW)] + [_full(a.shape) for a in ins[5:]],
        out_specs=row(d),
        out_shape=jax.ShapeDtypeStruct((n, d), F32),
        compiler_params=_params(("parallel",)),
        name="even_back",
    )(*ins)


def _group_scan(a, b):
    rows, width = a.shape
    a = a.reshape(rows // SUBLANES, SUBLANES, width)
    b = b.reshape(rows // SUBLANES, SUBLANES, width)
    t = lax.broadcasted_iota(jnp.int32, a.shape, 1)
    for s in (1, 2, 4):
        keep = t >= s
        ar = pltpu.roll(a, s, 1)
        br = pltpu.roll(b, s, 1)
        b = jnp.where(keep, a * br + b, b)
        a = jnp.where(keep, a * ar, a)
    return a.reshape(rows, width), b.reshape(rows, width)


def _odd_gate_dots(xc, wax_ref):
    xcb = xc.astype(BF16)
    rs, is_ = [], []
    for n in range(RNN_BLOCKS):
        ga = _dot(xcb[:, n * RNN_BW:(n + 1) * RNN_BW], wax_ref[n])
        rs.append(ga[:, :RNN_BW])
        is_.append(ga[:, RNN_BW:])
    return jnp.concatenate(rs, axis=-1), jnp.concatenate(is_, axis=-1)


def _odd_gates(xch, wax_ref, ba_ref, bx_ref, lam_ref):
    pre_r, pre_i = _odd_gate_dots(xch, wax_ref)
    tr = jnp.tanh(pre_r + ba_ref[...])
    ti = jnp.tanh(pre_i + bx_ref[...])
    lam = lam_ref[...]
    sp_neg_lam = jnp.maximum(-lam, 0.0) + jnp.log1p(jnp.exp(-jnp.abs(lam)))
    c2 = (-0.5 * LRU_C * LOG2E) * sp_neg_lam
    a = jnp.exp2(c2 * tr + c2)
    b = jnp.exp2(0.5 * jnp.log2(1.0 - a * a)) * xch * (ti + 1.0)
    return a, b


def _odd_prompt_kernel(x_ref, ln_ref, wu_ref, wg_ref, cw_ref, cb_ref, wax_ref, ba_ref, bx_ref, lam_ref,
                       wout_ref, y_ref, hl_ref, cv_ref, ush_scr, hc_scr, *, tt):
    tb = pl.program_id(1)
    ns = SUBLANES

    @pl.when(tb == 0)
    def _():
        ush_scr[0:ns, :] = jnp.zeros((ns, D_RNN), F32)
        hc_scr[...] = jnp.zeros_like(hc_scr)

    x = x_ref[0]
    xb = _rms(x, ln_ref[...]).astype(BF16)
    u = _dot(xb, wu_ref[...])
    gate = _dot(xb, wg_ref[...])
    ush_scr[ns:ns + tt, :] = u
    cw = cw_ref[...]
    xc = cb_ref[...] + u * cw[CONV_W - 1:CONV_W, :]
    for s in range(1, CONV_W):
        xc = xc + ush_scr[ns - s:ns - s + tt, :] * cw[CONV_W - 1 - s:CONV_W - s, :]
    ush_scr[0:ns, :] = ush_scr[tt:tt + ns, :]
    a, b = _odd_gates(xc, wax_ref, ba_ref, bx_ref, lam_ref)
    a, b = _group_scan(a, b)
    c = hc_scr[...]
    hs = []
    for g in range(tt // ns):
        hg = a[g * ns:(g + 1) * ns] * c + b[g * ns:(g + 1) * ns]
        hs.append(hg)
        c = hg[ns - 1:ns, :]
    hc_scr[...] = c
    h = jnp.concatenate(hs, axis=0)
    y_ref[0] = x + _dot((h * _half_silu(gate)).astype(BF16), wout_ref[...])

    @pl.when(tb == pl.num_programs(1) - 1)
    def _():
        hl_ref[0] = c
        cv_ref[0] = ush_scr[ns - (CONV_W - 1):ns, :]


def _odd_sample_kernel(x_ref, ext_ref, ln_ref, wu_ref, wg_ref, cw_ref, cb_ref, wax_ref, ba_ref, bx_ref, lam_ref,
                       wout_ref, y_ref, h_ref, u_ref):
    ns = SUBLANES
    x = x_ref[...]
    rows = x.shape[0]
    xb = _rms(x, ln_ref[...]).astype(BF16)
    u = _dot(xb, wu_ref[...])
    gate = _dot(xb, wg_ref[...])
    u_ref[...] = u
    ext = ext_ref[...]
    t = lax.broadcasted_iota(jnp.int32, (rows, D_RNN), 0) % ns
    cw = cw_ref[...]
    xc = cb_ref[...] + u * cw[CONV_W - 1:CONV_W, :]
    for s in range(1, CONV_W):
        prev = jnp.where(t < s, pltpu.roll(ext, rows + s - ns, 0), pltpu.roll(u, s, 0))
        xc = xc + prev * cw[CONV_W - 1 - s:CONV_W - s, :]
    a, b = _odd_gates(xc, wax_ref, ba_ref, bx_ref, lam_ref)
    h0_at0 = jnp.where(t == 0, pltpu.roll(ext, rows - (ns - CONV_W), 0), 0.0)
    _, h = _group_scan(a, b + a * h0_at0)
    h_ref[...] = h
    y_ref[...] = x + _dot((h * _half_silu(gate)).astype(BF16), wout_ref[...])


_ODD_WEIGHTS = ('ln', 'wu', 'wg', 'cw', 'cb', 'wax', 'ba', 'bx', 'lam', 'wout')


def _odd_prompt(x, w, tt):
    b, t, d = x.shape
    wl = [w[n] for n in _ODD_WEIGHTS]
    return pl.pallas_call(
        functools.partial(_odd_prompt_kernel, tt=tt),
        grid=(b, t // tt),
        in_specs=[pl.BlockSpec((1, tt, d), lambda i, j: (i, j, 0))] + [_resident(a.shape) for a in wl],
        out_specs=[pl.BlockSpec((1, tt, d), lambda i, j: (i, j, 0)),
                   pl.BlockSpec((1, 1, D_RNN), lambda i, j: (i, 0, 0)),
                   pl.BlockSpec((1, CONV_W - 1, D_RNN), lambda i, j: (i, 0, 0))],
        out_shape=[jax.ShapeDtypeStruct((b, t, d), F32), jax.ShapeDtypeStruct((b, 1, D_RNN), F32),
                   jax.ShapeDtypeStruct((b, CONV_W - 1, D_RNN), F32)],
        scratch_shapes=[pltpu.VMEM((tt + SUBLANES, D_RNN), F32), pltpu.VMEM((1, D_RNN), F32)],
        compiler_params=_params(("parallel", "arbitrary")),
        name="odd_prompt",
    )(x, *wl)


def _odd_sample(x2, ext, w, rows):
    n, d = x2.shape
    wl = [w[n_] for n_ in _ODD_WEIGHTS]
    row = lambda w_: pl.BlockSpec((rows, w_), lambda i: (i, 0))
    return pl.pallas_call(
        _odd_sample_kernel,
        grid=(n // rows,),
        in_specs=[row(d), row(D_RNN)] + [_resident(a.shape) for a in wl],
        out_specs=[row(d), row(D_RNN), row(D_RNN)],
        out_shape=[jax.ShapeDtypeStruct((n, d), F32), jax.ShapeDtypeStruct((n, D_RNN), F32),
                   jax.ShapeDtypeStruct((n, D_RNN), F32)],
        compiler_params=_params(("parallel",)),
        name="odd_sample",
    )(x2, ext, *wl)


def _pad_heads(v, n_used):
    lead = v.shape[:-1]
    v = v.reshape(lead + (MLA_HEADS, n_used))
    v = jnp.pad(v, [(0, 0)] * len(lead) + [(0, 0), (0, HEAD_PAD - n_used)])
    return v.reshape(lead + (MLA_PADW,))


def _rot_half(w):
    return jnp.concatenate([-w[..., HALF_ROPE:], w[..., :HALF_ROPE]], axis=-1)


def _swap_half_gain(g):
    return jnp.concatenate([g[:MLA_NOPE], g[MLA_NOPE + HALF_ROPE:], g[MLA_NOPE:MLA_NOPE + HALF_ROPE]])


def _pack_even(j, ln_even, w_in_even, gla_w_f2, gla_b_f, gla_out_norm, mla_q_norm, mla_kv_norm, mla_w_uq,
               mla_w_ukv, mla_qh_norm, mla_kh_norm, w_out_even):
    wi = w_in_even[j]
    d = wi.shape[0]
    o = 0
    seg = {}
    for name, width in (('qa', GLA_QK), ('ka', GLA_QK), ('va', GLA_VW), ('ga', GLA_VW), ('fa', GLA_RANK),
                        ('cq', Q_LORA), ('ckv', KV_LORA), ('kr', MLA_ROPE), ('gb', MLA_VW)):
        seg[name] = wi[:, o:o + width]
        o += width
    z = lambda n: jnp.zeros((d, n), wi.dtype)
    tail = MISC_W - ROPE_LANE0 - MLA_ROPE
    misc = jnp.concatenate([seg['fa'], z(ROPE_LANE0 - GLA_RANK), seg['kr'], z(tail)], 1)
    misc_rot = jnp.concatenate([z(ROPE_LANE0), _rot_half(seg['kr']), z(tail)], 1)
    win = jnp.concatenate([seg['qa'], seg['ka'], seg['va'], 0.5 * seg['ga'], seg['cq'], seg['ckv'],
                           0.5 * seg['gb'], misc, misc_rot], 1)
    wf2 = jnp.pad(gla_w_f2[j], ((0, MISC_W - GLA_RANK), (0, 0)))
    ukv = mla_w_ukv[j].reshape(KV_LORA, MLA_HEADS, MLA_NOPE + MLA_V)
    wuk = ukv[:, :, :MLA_NOPE].reshape(KV_LORA, MLA_NOPEW)
    wuv = ukv[:, :, MLA_NOPE:].reshape(KV_LORA, MLA_VW)
    uq = mla_w_uq[j].reshape(Q_LORA, MLA_HEADS, MLA_QK)
    uq_rot = jnp.concatenate([jnp.zeros_like(uq[..., :MLA_NOPE]), _rot_half(uq[..., MLA_NOPE:])], -1)
    khp = jnp.pad(mla_kh_norm[j], (0, HEAD_PAD - MLA_QK))
    return {
        'ln': ln_even[j][None, :],
        'win': win.astype(BF16),
        'wf2': wf2.astype(BF16),
        'bf': gla_b_f[j][None, :],
        'qn': mla_q_norm[j][None, :],
        'kvn': mla_kv_norm[j][None, :],
        'wuq': _pad_heads(mla_w_uq[j], MLA_QK).astype(BF16),
        'wuqr': _pad_heads(uq_rot.reshape(Q_LORA, MLA_HEADS * MLA_QK), MLA_QK).astype(BF16),
        'wukp': _pad_heads(wuk, MLA_NOPE).astype(BF16),
        'wukt': wuk.T.astype(BF16),
        'wuv': wuv.astype(BF16),
        'wuvt': wuv.T.astype(BF16),
        'qh': jnp.pad(mla_qh_norm[j], (0, HEAD_PAD - MLA_QK))[None, :],
        'qhs': jnp.pad(_swap_half_gain(mla_qh_norm[j]), (0, HEAD_PAD - MLA_QK))[None, :],
        'kh': khp[None, :],
        'khs': jnp.pad(_swap_half_gain(mla_kh_norm[j]), (0, HEAD_PAD - MLA_QK))[None, :],
        'khg': jnp.tile(khp, MLA_HEADS)[None, :],
        'grc': mla_kh_norm[j][MLA_NOPE:][:, None],
        'on': gla_out_norm[j][None, :],
        'wout_a': w_out_even[j][:GLA_VW].astype(BF16),
        'wout_b': w_out_even[j][GLA_VW:].astype(BF16),
    }


def _pack_odd(j, ln_odd, w_in_odd, conv_w, conv_b, rg_w_a, rg_b_a, rg_w_x, rg_b_x, rg_lambda, w_out_odd):
    return {
        'ln': ln_odd[j][None, :],
        'wu': w_in_odd[j][:, :D_RNN].astype(BF16),
        'wg': (0.5 * w_in_odd[j][:, D_RNN:]).astype(BF16),
        'cw': 0.5 * conv_w[j],
        'cb': 0.5 * conv_b[j][None, :],
        'wax': jnp.concatenate([rg_w_a[j], rg_w_x[j]], axis=-1).astype(BF16),
        'ba': 0.5 * rg_b_a[j][None, :],
        'bx': 0.5 * rg_b_x[j][None, :],
        'lam': rg_lambda[j][None, :],
        'wout': w_out_odd[j].astype(BF16),
    }


def _rope_angles(pos):
    inv_freq = ROPE_THETA ** (-jnp.arange(HALF_ROPE, dtype=F32) / HALF_ROPE)
    ang = pos.astype(F32)[:, None] * inv_freq[None, :]
    return jnp.cos(ang), jnp.sin(ang)


def _rope_tile_tables(pos):
    cos, sin = _rope_angles(pos)
    n = pos.shape[0]
    tail = LANES - ROPE_LANE0 - MLA_ROPE
    c = jnp.concatenate([jnp.ones((n, ROPE_LANE0), F32), cos, cos, jnp.ones((n, tail), F32)], 1)
    s = jnp.concatenate([jnp.zeros((n, ROPE_LANE0), F32), sin, sin, jnp.zeros((n, tail), F32)], 1)
    return c, s


def _rope_pair_tables_t(pos):
    cos, sin = _rope_angles(pos)
    return jnp.concatenate([cos, cos], 1).T, jnp.concatenate([sin, sin], 1).T


def _pick(n, prefs):
    for p in prefs:
        if n % p == 0:
            return p
    return n


def kernel(x_prompt, x_sample, state_gla, cache_ckv, cache_kr, state_rglru_h, state_rglru_conv, page_table,
           ln_even, w_in_even, gla_w_f2, gla_b_f, gla_out_norm, mla_q_norm, mla_kv_norm, mla_w_uq, mla_w_ukv,
           mla_qh_norm, mla_kh_norm, w_out_even, ln_odd, w_in_odd, conv_w, conv_b, rg_w_a, rg_b_a, rg_w_x,
           rg_b_x, rg_lambda, w_out_odd):
    bp, tp, d = x_prompt.shape
    bs, ts, _ = x_sample.shape
    n_pages = page_table.shape[1]
    past_len = n_pages * PAGE_SIZE
    depth = ln_even.shape[0] + ln_odd.shape[0]
    assert ts == SUBLANES, "sample group is handled as one 8-row group per sequence"
    assert MLA_NOPE == MLA_V

    tm_p = _pick(tp, (256, 128, 64, 32, 16, 8))
    tm_s = _pick(bs * ts, (256, 128, 64, 32, 16, 8))
    tm_back = _pick(tp, (512, 256, 128, 64, 32, 16, 8))
    nseq_p = _pick(bp, (2, 1))
    nseq_s = _pick(bs, (8, 4, 2, 1))
    chunk_p = min(GLA_CHUNK, tp)
    tblk_p = _pick(tp, (256, 128, 64)) if tp >= GLA_CHUNK else tp
    tq = _pick(tp, (512, 256, 128))
    tt_odd = _pick(tp, (256, 128, 64))
    n_pg = _pick(n_pages, (PAGED_KEYS_PER_STEP // PAGE_SIZE, 32, 16, 8, 4, 2, 1))

    tabs_p = _rope_tile_tables(jnp.arange(tp))
    pos_s = past_len + jnp.arange(ts)
    tabs_s = tuple(jnp.tile(a, (tm_s // ts, 1)) for a in _rope_tile_tables(pos_s))
    pair_new = _rope_pair_tables_t(pos_s)
    pair_past = _rope_pair_tables_t(jnp.arange(past_len))
    pt_flat = page_table.reshape(-1).astype(jnp.int32)
    cache_krt = jnp.swapaxes(cache_kr, -1, -2)

    yp = x_prompt.reshape(bp * tp, d)
    ys = x_sample.reshape(bs * ts, d)
    outs = {k: [] for k in ('gla_p', 'ckv_p', 'kr_p', 'rh_p', 'rc_p', 'gla_s', 'ckv_s', 'kr_s', 'rh_s', 'rc_s')}
    for layer in range(depth):
        j = layer // 2
        if layer % 2 == 0:
            w = _pack_even(j, ln_even, w_in_even, gla_w_f2, gla_b_f, gla_out_norm, mla_q_norm, mla_kv_norm,
                           mla_w_uq, mla_w_ukv, mla_qh_norm, mla_kh_norm, w_out_even)
            qa, ka, va, lf, ga, gb, q, k, v, ckvn, misc = _even_front(yp, w, tabs_p, tm_p, tp)
            r3 = lambda a: a.reshape(bp, tp, a.shape[-1])
            oa, st = _gla_scan(r3(qa), r3(ka), r3(va), r3(lf), None, chunk_p, tblk_p, nseq_p)
            ob = _flash_attn(r3(q), r3(k), v, tq, FLASH_HEADS_PER_STEP)
            yp = _even_back(yp, oa.reshape(bp * tp, GLA_VW), ga, ob.reshape(bp * tp, MLA_VW), gb, w, tm_back)
            outs['gla_p'].append(st.reshape(bp, GLA_HEADS, GLA_DK, GLA_DV))
            outs['ckv_p'].append(ckvn.reshape(bp, tp, KV_LORA))
            outs['kr_p'].append(misc[:, ROPE_LANE0:ROPE_LANE0 + MLA_ROPE].reshape(bp, tp, MLA_ROPE))
            qa, ka, va, lf, ga, gb, q, k, v, ckvn, misc = _even_front(ys, w, tabs_s, tm_s, ts)
            r3 = lambda a: a.reshape(bs, ts, a.shape[-1])
            s0t = state_gla[j].reshape(bs, GLA_HEADS // 2, 2 * GLA_DK, GLA_DV)
            oa, st = _gla_scan(r3(qa), r3(ka), r3(va), r3(lf), s0t, ts, ts, nseq_s)
            krs = misc[:, ROPE_LANE0:ROPE_LANE0 + MLA_ROPE].reshape(bs, ts, MLA_ROPE)
            q4 = q.reshape(bs, ts, MLA_HEADS, HEAD_PAD)
            qrope = jnp.swapaxes(q4[..., ROPE_LANE0:ROPE_LANE0 + MLA_ROPE], 1, 2)
            qrope = qrope.reshape(bs, MLA_HEADS * ts, MLA_ROPE)
            qrr = jnp.concatenate([qrope, qrope[..., HALF_ROPE:], -qrope[..., :HALF_ROPE]], axis=-1)
            ob = _paged_attn(j, pt_flat, n_pages, r3(q), qrr, r3(ckvn), jnp.swapaxes(krs, 1, 2), pair_new,
                             pair_past, w, cache_ckv, cache_krt, n_pg)
            ys = _even_back(ys, oa.reshape(bs * ts, GLA_VW), ga, ob.reshape(bs * ts, MLA_VW), gb, w, tm_s)
            outs['gla_s'].append(st.reshape(bs, GLA_HEADS, GLA_DK, GLA_DV))
            outs['ckv_s'].append(ckvn.reshape(bs, ts, KV_LORA))
            outs['kr_s'].append(krs)
        else:
            w = _pack_odd(j, ln_odd, w_in_odd, conv_w, conv_b, rg_w_a, rg_b_a, rg_w_x, rg_b_x, rg_lambda,
                          w_out_odd)
            y3, hl, cv = _odd_prompt(yp.reshape(bp, tp, d), w, tt_odd)
            yp = y3.reshape(bp * tp, d)
            outs['rh_p'].append(hl.reshape(bp, D_RNN))
            outs['rc_p'].append(cv)
            ext = jnp.concatenate([jnp.zeros((bs, ts - CONV_W, D_RNN), F32), state_rglru_h[j][:, None, :],
                                   state_rglru_conv[j]], axis=1).reshape(bs * ts, D_RNN)
            ys, hfull, ufull = _odd_sample(ys, ext, w, tm_s)
            outs['rh_s'].append(hfull.reshape(bs, ts, D_RNN)[:, ts - 1])
            outs['rc_s'].append(ufull.reshape(bs, ts, D_RNN)[:, ts - (CONV_W - 1):])
    st_ = lambda name: jnp.stack(outs[name])
    return (yp.reshape(bp, tp, d), ys.reshape(bs, ts, d),
            st_('gla_p'), st_('ckv_p'), st_('kr_p'), st_('rh_p'), st_('rc_p'),
            st_('gla_s'), st_('ckv_s'), st_('kr_s'), st_('rh_s'), st_('rc_s'))
```

```python
import functools

import jax
import jax.numpy as jnp
from jax import lax
from jax.experimental import pallas as pl
from jax.experimental.pallas import tpu as pltpu

F32 = jnp.float32
BF16 = jnp.bfloat16
EPS = 1e-6

GLA_HEADS = 4
GLA_DK = 64
GLA_DV = 128
GLA_RANK = 16
GLA_TAU = 16.0
GLA_CHUNK = 64
GLA_QK = GLA_HEADS * GLA_DK
GLA_VW = GLA_HEADS * GLA_DV
MLA_HEADS = 8
MLA_NOPE = 64
MLA_ROPE = 32
MLA_V = 64
MLA_QK = MLA_NOPE + MLA_ROPE
MLA_VW = MLA_HEADS * MLA_V
MLA_NOPEW = MLA_HEADS * MLA_NOPE
Q_LORA = 384
KV_LORA = 256
ROPE_THETA = 10000.0
PAGE_SIZE = 128
D_RNN = 1280
RNN_BLOCKS = 10
RNN_BW = D_RNN // RNN_BLOCKS
CONV_W = 4
LRU_C = 8.0

LANES = 128
SUBLANES = 8
HEAD_PAD = LANES
MLA_PADW = MLA_HEADS * HEAD_PAD
HALF_ROPE = MLA_ROPE // 2
ROPE_LANE0 = MLA_NOPE
MISC_W = LANES
LOG2E = 1.4426950408889634
Q_SCALE = MLA_QK ** -0.5 * LOG2E
FLASH_HEADS_PER_STEP = 8
FLASH_TK = 512
EXP_CLAMP = 80.0
NEG_BIG = -1e30
VMEM_LIMIT = 56 * 1024 * 1024
PAGED_KEYS_PER_STEP = 8192

NT_DIMS = (((1,), (1,)), ((), ()))
TN_DIMS = (((0,), (0,)), ((), ()))


def _dot(a, b):
    return jnp.dot(a, b, preferred_element_type=F32)


def _dot_nt(a, b):
    return lax.dot_general(a, b, NT_DIMS, preferred_element_type=F32)


def _dot_tn(a, b):
    return lax.dot_general(a, b, TN_DIMS, preferred_element_type=F32)


def _rms(x, g, n=None):
    n = x.shape[-1] if n is None else n
    ss = jnp.sum(x * x, axis=-1, keepdims=True) * (1.0 / n)
    return x * lax.rsqrt(ss + EPS) * g


def _half_silu(gh):
    return gh * jnp.tanh(gh) + gh


def _log_sigmoid(x):
    return jnp.minimum(x, 0.0) - jnp.log1p(jnp.exp(-jnp.abs(x)))


def _params(sem):
    return pltpu.CompilerParams(dimension_semantics=sem, vmem_limit_bytes=VMEM_LIMIT)


def _full(shape):
    nd = len(shape)
    return pl.BlockSpec(shape, lambda *_: (0,) * nd)


def _resident(shape):
    nd = len(shape)
    return pl.BlockSpec(shape, lambda *_: (0,) * nd, pipeline_mode=pl.Buffered(1))


def _even_front_kernel(x_ref, ln_ref, win_ref, wf2_ref, bf_ref, qn_ref, kvn_ref, wuq_ref, wuqr_ref, wuk_ref,
                       wuv_ref, qh_ref, qhs_ref, kh_ref, khs_ref, c_ref, s_ref,
                       qa_ref, ka_ref, va_ref, lf_ref, ga_ref, gb_ref, q_ref, k_ref, v_ref,
                       ckv_ref, misc_ref):
    x = x_ref[...]
    xn = _rms(x, ln_ref[...])
    z = _dot(xn.astype(BF16), win_ref[...])
    o = 0
    qa_ref[...] = z[:, o:o + GLA_QK] * (GLA_DK ** -0.5); o += GLA_QK
    ka_ref[...] = z[:, o:o + GLA_QK]; o += GLA_QK
    va_ref[...] = z[:, o:o + GLA_VW]; o += GLA_VW
    ga_ref[...] = _half_silu(z[:, o:o + GLA_VW]); o += GLA_VW
    cq = z[:, o:o + Q_LORA]; o += Q_LORA
    ckv = z[:, o:o + KV_LORA]; o += KV_LORA
    gb_ref[...] = _half_silu(z[:, o:o + MLA_VW]); o += MLA_VW
    misc = z[:, o:o + MISC_W]; o += MISC_W
    misc_rot = z[:, o:o + MISC_W]
    misc_ref[...] = misc

    f = _dot(misc.astype(BF16), wf2_ref[...]) + bf_ref[...]
    lf_ref[...] = _log_sigmoid(f) * (1.0 / GLA_TAU)

    c = c_ref[...]
    s = s_ref[...]

    cqb = _rms(cq, qn_ref[...]).astype(BF16)
    ckvn = _rms(ckv, kvn_ref[...])
    ckv_ref[...] = ckvn
    cb = ckvn.astype(BF16)
    qraw = _dot(cqb, wuq_ref[...])
    qrot = _dot(cqb, wuqr_ref[...])
    kraw = _dot(cb, wuk_ref[...])
    v_ref[...] = _dot_nt(wuv_ref[...], cb).astype(BF16)
    gc = qh_ref[...] * c * Q_SCALE
    gs = qhs_ref[...] * s * Q_SCALE
    for h in range(MLA_HEADS):
        hs = slice(h * HEAD_PAD, (h + 1) * HEAD_PAD)
        t = qraw[:, hs]
        r = lax.rsqrt(jnp.sum(t * t, axis=-1, keepdims=True) * (1.0 / MLA_QK) + EPS)
        q_ref[:, hs] = ((t * gc + qrot[:, hs] * gs) * r).astype(BF16)

    lane = lax.broadcasted_iota(jnp.int32, misc.shape, 1)
    krt = jnp.where(lane >= ROPE_LANE0, misc, 0.0)
    gc = kh_ref[...] * c
    rot_gs = misc_rot * (khs_ref[...] * s)
    for h in range(MLA_HEADS):
        hs = slice(h * HEAD_PAD, (h + 1) * HEAD_PAD)
        t = kraw[:, hs] + krt
        r = lax.rsqrt(jnp.sum(t * t, axis=-1, keepdims=True) * (1.0 / MLA_QK) + EPS)
        k_ref[:, hs] = ((t * gc + rot_gs) * r).astype(BF16)


def _even_front(x2, w, tabs, tm, rows_per_seq):
    n, d = x2.shape
    nblk_per_seq = rows_per_seq // tm if rows_per_seq >= tm else None
    row = lambda w_: pl.BlockSpec((tm, w_), lambda i: (i, 0))
    if nblk_per_seq is not None:
        tab = pl.BlockSpec((tm, LANES), lambda i: (i % nblk_per_seq, 0))
    else:
        tab = pl.BlockSpec((tm, LANES), lambda i: (0, 0))
    ins = [x2, w['ln'], w['win'], w['wf2'], w['bf'], w['qn'], w['kvn'], w['wuq'], w['wuqr'], w['wukp'],
           w['wuvt'], w['qh'], w['qhs'], w['kh'], w['khs'], tabs[0], tabs[1]]
    in_specs = [row(d)] + [_resident(a.shape) for a in ins[1:15]] + [tab, tab]
    outs = [(GLA_QK, F32), (GLA_QK, F32), (GLA_VW, F32), (GLA_QK, F32), (GLA_VW, F32), (MLA_VW, F32),
            (MLA_PADW, BF16), (MLA_PADW, BF16), None, (KV_LORA, F32), (MISC_W, F32)]
    vt_spec = pl.BlockSpec((MLA_VW, tm), lambda i: (0, i))
    vt_shape = jax.ShapeDtypeStruct((MLA_VW, n), BF16)
    return pl.pallas_call(
        _even_front_kernel,
        grid=(n // tm,),
        in_specs=in_specs,
        out_specs=[vt_spec if o_ is None else row(o_[0]) for o_ in outs],
        out_shape=[vt_shape if o_ is None else jax.ShapeDtypeStruct((n, o_[0]), o_[1]) for o_ in outs],
        compiler_params=_params(("parallel",)),
        name="even_front",
    )(*ins)


def _gla_kernel(*refs, chunk, n_chunks, n_seq, has_s0):
    if has_s0:
        q_ref, k_ref, v_ref, g_ref, s0_ref, o_ref, st_ref, s_scr = refs
    else:
        q_ref, k_ref, v_ref, g_ref, o_ref, st_ref, s_scr = refs
    tb = pl.program_id(1)

    @pl.when(tb == 0)
    def _():
        if has_s0:
            s_scr[...] = s0_ref[...]
        else:
            s_scr[...] = jnp.zeros_like(s_scr)

    c_ = chunk
    row = lax.broadcasted_iota(jnp.int32, (c_, c_), 0)
    col = lax.broadcasted_iota(jnp.int32, (c_, c_), 1)
    tri = row >= col
    tri_bf = jnp.where(tri, 1.0, 0.0).astype(BF16)
    mid_row = c_ // 2 - 1
    lane_head = lax.broadcasted_iota(jnp.int32, (c_, LANES), 1) // GLA_DK
    zero_bf = jnp.zeros((c_, LANES), BF16)
    for c in range(n_chunks):
        sl = slice(c * c_, (c + 1) * c_)
        cums = []
        for b_ in range(n_seq):
            g = g_ref[b_, sl, :]
            g_hi = g.astype(BF16)
            g_lo = (g - g_hi.astype(F32)).astype(BF16)
            cums.append(_dot(tri_bf, g_hi) + _dot(tri_bf, g_lo))
        work = []
        for b_ in range(n_seq):
            cum = cums[b_]
            tot = cum[c_ - 1:c_, :]
            mid = cum[mid_row:mid_row + 1, :]
            q = q_ref[b_, sl, :]
            k = k_ref[b_, sl, :]
            qd = (q * jnp.exp(cum)).astype(BF16)
            qm = (q * jnp.exp(jnp.minimum(cum - mid, EXP_CLAMP))).astype(BF16)
            km = (k * jnp.exp(jnp.minimum(mid - cum, EXP_CLAMP))).astype(BF16)
            kd = (k * jnp.exp(tot - cum)).astype(BF16)
            etot = jnp.exp(tot)
            for pr in range(GLA_HEADS // 2):
                ts = slice(pr * LANES, (pr + 1) * LANES)
                heads = []
                for hh in range(2):
                    h = 2 * pr + hh
                    mine = lane_head == hh
                    vs = slice(h * GLA_DV, (h + 1) * GLA_DV)
                    vh = v_ref[b_, sl, vs].astype(BF16)
                    a = _dot_nt(jnp.where(mine, qm[:, ts], zero_bf), km[:, ts])
                    heads.append((vs, vh, a, jnp.where(mine, qd[:, ts], zero_bf),
                                  jnp.where(mine, kd[:, ts], zero_bf)))
                work.append((b_, pr, etot[:, ts], heads))
        upds = [_dot_tn(jnp.concatenate([hd[4] for hd in heads], axis=0),
                        jnp.concatenate([hd[1] for hd in heads], axis=0)) for _, _, _, heads in work]
        for (b_, pr, etot_t, heads), upd in zip(work, upds):
            st = s_scr[b_, pr]
            st_bf = st.astype(BF16)
            for vs, vh, a, qdm, _ in heads:
                a = jnp.where(tri, a, 0.0).astype(BF16)
                if c_ % GLA_DK == 0:
                    o_ref[b_, sl, vs] = _dot(jnp.concatenate([qdm, a], axis=1),
                                             jnp.concatenate([st_bf, vh], axis=0))
                else:
                    o_ref[b_, sl, vs] = _dot(qdm, st_bf) + _dot(a, vh)
            decay = jnp.transpose(jnp.broadcast_to(etot_t, (LANES, LANES)))
            s_scr[b_, pr] = st * decay + upd

    @pl.when(tb == pl.num_programs(1) - 1)
    def _():
        st_ref[...] = s_scr[...]


def _gla_scan(qa, ka, va, lf, s0t, chunk, tblk, n_seq):
    b, t, _ = qa.shape
    n_chunks = tblk // chunk
    has_s0 = s0t is not None
    seq = lambda w_: pl.BlockSpec((n_seq, tblk, w_), lambda i, j: (i, j, 0))
    st_spec = pl.BlockSpec((n_seq, GLA_HEADS // 2, 2 * GLA_DK, GLA_DV), lambda i, j: (i, 0, 0, 0))
    ins = [qa, ka, va, lf] + ([s0t] if has_s0 else [])
    in_specs = [seq(GLA_QK), seq(GLA_QK), seq(GLA_VW), seq(GLA_QK)] + ([st_spec] if has_s0 else [])
    return pl.pallas_call(
        functools.partial(_gla_kernel, chunk=chunk, n_chunks=n_chunks, n_seq=n_seq, has_s0=has_s0),
        grid=(b // n_seq, t // tblk),
        in_specs=in_specs,
        out_specs=[seq(GLA_VW), st_spec],
        out_shape=[jax.ShapeDtypeStruct((b, t, GLA_VW), F32),
                   jax.ShapeDtypeStruct((b, GLA_HEADS // 2, 2 * GLA_DK, GLA_DV), F32)],
        scratch_shapes=[pltpu.VMEM((n_seq, GLA_HEADS // 2, 2 * GLA_DK, GLA_DV), F32)],
        compiler_params=_params(("parallel", "arbitrary")),
        name="gla_scan",
    )(*ins)


def _flash_kernel(q_ref, k_ref, vt_ref, o_ref, m_scr, l_scr, acc_scr, *, tq, tk, n_heads):
    qi = pl.program_id(2)
    m_scr[...] = jnp.full_like(m_scr, NEG_BIG)
    l_scr[...] = jnp.zeros_like(l_scr)
    acc_scr[...] = jnp.zeros_like(acc_scr)

    def block(ki, diagonal):
        ks = pl.ds(pl.multiple_of(ki * tk, tk), tk)
        sts = []
        for h in range(n_heads):
            kh = k_ref[0, ks, h * HEAD_PAD:(h + 1) * HEAD_PAD]
            qh = q_ref[0, :, h * HEAD_PAD:(h + 1) * HEAD_PAD]
            sts.append(_dot_nt(kh, qh))
        for h in range(n_heads):
            st = sts[h]
            if diagonal:
                key = lax.broadcasted_iota(jnp.int32, (tk, tq), 0) + ki * tk
                qry = lax.broadcasted_iota(jnp.int32, (tk, tq), 1) + qi * tq
                st = jnp.where(key <= qry, st, NEG_BIG)
            m_prev = m_scr[h]
            m_new = jnp.maximum(m_prev, jnp.max(st, axis=0, keepdims=True))
            alpha = jnp.exp2(m_prev - m_new)
            p = jnp.exp2(st - m_new)
            l_scr[h] = alpha * l_scr[h] + jnp.sum(p, axis=0, keepdims=True)
            m_scr[h] = m_new
            vt = vt_ref[h * MLA_V:(h + 1) * MLA_V, ks]
            acc_scr[h] = alpha * acc_scr[h] + _dot(vt, p.astype(BF16))

    def body(ki, carry):
        block(ki, False)
        return carry

    per_q = tq // tk
    lax.fori_loop(0, qi * per_q, body, 0)
    for j in range(per_q):
        block(qi * per_q + j, True)
    for pr in range(n_heads // 2):
        ot = jnp.concatenate([acc_scr[2 * pr] / l_scr[2 * pr], acc_scr[2 * pr + 1] / l_scr[2 * pr + 1]], axis=0)
        o_ref[0, :, pr * LANES:(pr + 1) * LANES] = jnp.transpose(ot)


def _flash_attn(q, k, vt, tq, n_heads):
    b, t, _ = q.shape
    nq = t // tq
    ng = MLA_HEADS // n_heads
    return pl.pallas_call(
        functools.partial(_flash_kernel, tq=tq, tk=min(FLASH_TK, tq), n_heads=n_heads),
        grid=(b, ng, nq),
        in_specs=[pl.BlockSpec((1, tq, n_heads * HEAD_PAD), lambda bi, g, i: (bi, i, g)),
                  pl.BlockSpec((1, t, n_heads * HEAD_PAD), lambda bi, g, i: (bi, 0, g)),
                  pl.BlockSpec((n_heads * MLA_V, t), lambda bi, g, i: (g, bi))],
        out_specs=pl.BlockSpec((1, tq, n_heads * MLA_V), lambda bi, g, i: (bi, i, g)),
        out_shape=jax.ShapeDtypeStruct((b, t, MLA_VW), F32),
        scratch_shapes=[pltpu.VMEM((n_heads, 1, tq), F32), pltpu.VMEM((n_heads, 1, tq), F32),
                        pltpu.VMEM((n_heads, MLA_V, tq), F32)],
        compiler_params=_params(("parallel", "parallel", "arbitrary")),
        name="flash_attn",
    )(q, k, vt)


def _paged_kernel(pt_ref, q_ref, qrr_ref, latn_ref, krnt_ref, cosn_ref, sinn_ref, cos_ref, sin_ref,
                  khg_ref, grc_ref, wukp_ref, wukt_ref, wuv_ref, ckv_hbm, krt_hbm,
                  o_ref, lat_buf, krt_buf, sem, lhs_scr, m_scr, l_scr, acc_scr, *, layer, n_pg, t_new):
    p_idx = pl.program_id(1)
    steps = pl.num_programs(1)
    step = pl.program_id(0) * steps + p_idx
    total = pl.num_programs(0) * steps
    slot = lax.rem(step, 2)
    nq = MLA_HEADS * t_new
    tt = n_pg * PAGE_SIZE

    def page_copies(st, sl):
        cps = []
        for g in range(n_pg):
            page = pt_ref[st * n_pg + g]
            rows = pl.ds(g * PAGE_SIZE, PAGE_SIZE)
            cps.append(pltpu.make_async_copy(ckv_hbm.at[layer, page], lat_buf.at[sl, rows, :], sem.at[sl, 0]))
            cps.append(pltpu.make_async_copy(krt_hbm.at[layer, page], krt_buf.at[sl, :, rows], sem.at[sl, 1]))
        return cps

    @pl.when(step == 0)
    def _():
        for cp in page_copies(0, 0):
            cp.start()

    @pl.when(step + 1 < total)
    def _():
        for cp in page_copies(step + 1, 1 - slot):
            cp.start()

    for cp in page_copies(step, slot):
        cp.wait()

    @pl.when(p_idx == 0)
    def _():
        qt = q_ref[0].astype(F32)
        qrep = jnp.concatenate([qt] * MLA_HEADS, axis=0)
        r_ = lax.broadcasted_iota(jnp.int32, qrep.shape, 0) // t_new
        c_ = lax.broadcasted_iota(jnp.int32, qrep.shape, 1) // HEAD_PAD
        qbd = jnp.where(r_ == c_, qrep * khg_ref[...], 0.0)
        lhs_scr[0:nq, :] = _dot_nt(qbd.astype(BF16), wukp_ref[...]).astype(BF16)
        lhs_scr[nq:, :] = wukt_ref[...]
        m_scr[...] = jnp.full_like(m_scr, NEG_BIG)
        l_scr[...] = jnp.zeros_like(l_scr)
        acc_scr[...] = jnp.zeros_like(acc_scr)

    grc = grc_ref[...]
    qrr = qrr_ref[0]

    def attend(cb, kt, cost, sint, mask):
        n = cb.shape[0]
        x = kt * grc
        xx = jnp.concatenate([x * cost, x * sint], axis=0).astype(BF16)
        big = _dot_nt(lhs_scr[...], cb)
        rope = _dot(qrr, xx)
        k2 = jnp.sum(kt * kt, axis=0, keepdims=True)
        rows = []
        for h in range(MLA_HEADS):
            blk = big[nq + h * MLA_NOPE:nq + (h + 1) * MLA_NOPE]
            ssq = jnp.sum(blk * blk, axis=0, keepdims=True) + k2
            rows.append(jnp.broadcast_to(lax.rsqrt(ssq * (1.0 / MLA_QK) + EPS), (t_new, n)))
        s = (big[0:nq] + rope) * jnp.concatenate(rows, axis=0)
        if mask is not None:
            s = jnp.where(mask, s, NEG_BIG)
        m_prev = m_scr[...]
        m_new = jnp.maximum(m_prev, jnp.max(s, axis=-1, keepdims=True))
        alpha = jnp.exp2(m_prev - m_new)
        p = jnp.exp2(s - m_new[:, 0:1])
        l_scr[...] = alpha * l_scr[...] + jnp.sum(p, axis=-1, keepdims=True)
        m_scr[...] = m_new
        acc_scr[...] = alpha[:, 0:1] * acc_scr[...] + _dot(p.astype(BF16), cb)

    ks = pl.ds(pl.multiple_of(p_idx * tt, tt), tt)
    attend(lat_buf[slot].astype(BF16), krt_buf[slot], cos_ref[:, ks], sin_ref[:, ks], None)

    @pl.when(p_idx == pl.num_programs(1) - 1)
    def _():
        qi = lax.broadcasted_iota(jnp.int32, (nq, t_new), 0) % t_new
        kj = lax.broadcasted_iota(jnp.int32, (nq, t_new), 1)
        attend(latn_ref[0].astype(BF16), krnt_ref[0], cosn_ref[...], sinn_ref[...], kj <= qi)
        out_lat = acc_scr[...] / l_scr[:, 0:1]
        er = lax.broadcasted_iota(jnp.int32, (nq, MLA_VW), 0) // t_new
        ec = lax.broadcasted_iota(jnp.int32, (nq, MLA_VW), 1) // MLA_V
        full = jnp.where(er == ec, _dot(out_lat.astype(BF16), wuv_ref[...]), 0.0)
        o = full[0:t_new]
        for h in range(1, MLA_HEADS):
            o = o + full[h * t_new:(h + 1) * t_new]
        o_ref[0] = o


def _paged_attn(layer, page_table_flat, n_pages, q3, qrr, latn, krnt, tabs_new, tabs_past, w,
                cache_ckv, cache_krt, n_pg):
    b, t_new, _ = q3.shape
    nq = MLA_HEADS * t_new
    steps = n_pages // n_pg
    tt = n_pg * PAGE_SIZE

    def per_b(shape):
        nd = len(shape)
        return pl.BlockSpec((1,) + shape[1:], lambda bi, p, pt: (bi,) + (0,) * (nd - 1))

    def const(shape):
        nd = len(shape)
        return pl.BlockSpec(shape, lambda bi, p, pt: (0,) * nd, pipeline_mode=pl.Buffered(1))

    hbm = pl.BlockSpec(memory_space=pl.ANY)
    ins = [q3, qrr, latn, krnt, tabs_new[0], tabs_new[1], tabs_past[0], tabs_past[1],
           w['khg'], w['grc'], w['wukp'], w['wukt'], w['wuv'], cache_ckv, cache_krt]
    in_specs = [per_b(a.shape) for a in ins[:4]] + [const(a.shape) for a in ins[4:13]] + [hbm, hbm]
    grid_spec = pltpu.PrefetchScalarGridSpec(
        num_scalar_prefetch=1,
        grid=(b, steps),
        in_specs=in_specs,
        out_specs=pl.BlockSpec((1, t_new, MLA_VW), lambda bi, p, pt: (bi, 0, 0)),
        scratch_shapes=[pltpu.VMEM((2, tt, KV_LORA), F32), pltpu.VMEM((2, MLA_ROPE, tt), F32),
                        pltpu.SemaphoreType.DMA((2, 2)),
                        pltpu.VMEM((nq + MLA_NOPEW, KV_LORA), BF16), pltpu.VMEM((nq, LANES), F32),
                        pltpu.VMEM((nq, LANES), F32), pltpu.VMEM((nq, KV_LORA), F32)],
    )
    return pl.pallas_call(
        functools.partial(_paged_kernel, layer=layer, n_pg=n_pg, t_new=t_new),
        grid_spec=grid_spec,
        out_shape=jax.ShapeDtypeStruct((b, t_new, MLA_VW), F32),
        compiler_params=_params(("arbitrary", "arbitrary")),
        name="paged_attn",
    )(page_table_flat, *ins)


def _even_back_kernel(x_ref, oa_ref, ga_ref, ob_ref, gb_ref, on_ref, wa_ref, wb_ref, y_ref):
    oa = oa_ref[...]
    on = on_ref[...]
    parts = []
    for h in range(GLA_HEADS):
        parts.append(_rms(oa[:, h * GLA_DV:(h + 1) * GLA_DV], on))
    a = jnp.concatenate(parts, axis=-1) * ga_ref[...]
    b_ = ob_ref[...] * gb_ref[...]
    y_ref[...] = x_ref[...] + _dot(a.astype(BF16), wa_ref[...]) + _dot(b_.astype(BF16), wb_ref[...])


def _even_back(x2, oa, ga, ob, gb, w, tm):
    n, d = x2.shape
    row = lambda w_: pl.BlockSpec((tm, w_), lambda i: (i, 0))
    ins = [x2, oa, ga, ob, gb, w['on'], w['wout_a'], w['wout_b']]
    return pl.pallas_call(
        _even_back_kernel,
        grid=(n // tm,),
        in_specs=[row(d), row(GLA_VW), row(GLA_VW), row(MLA_VW), row(MLA_VW)] + [_full(a.shape) for a in ins[5:]],
        out_specs=row(d),
        out_shape=jax.ShapeDtypeStruct((n, d), F32),
        compiler_params=_params(("parallel",)),
        name="even_back",
    )(*ins)


def _group_scan(a, b):
    rows, width = a.shape
    a = a.reshape(rows // SUBLANES, SUBLANES, width)
    b = b.reshape(rows // SUBLANES, SUBLANES, width)
    t = lax.broadcasted_iota(jnp.int32, a.shape, 1)
    for s in (1, 2, 4):
        keep = t >= s
        ar = pltpu.roll(a, s, 1)
        br = pltpu.roll(b, s, 1)
        b = jnp.where(keep, a * br + b, b)
        a = jnp.where(keep, a * ar, a)
    return a.reshape(rows, width), b.reshape(rows, width)


def _odd_gate_dots(xc, wax_ref):
    xcb = xc.astype(BF16)
    rs, is_ = [], []
    for n in range(RNN_BLOCKS):
        ga = _dot(xcb[:, n * RNN_BW:(n + 1) * RNN_BW], wax_ref[n])
        rs.append(ga[:, :RNN_BW])
        is_.append(ga[:, RNN_BW:])
    return jnp.concatenate(rs, axis=-1), jnp.concatenate(is_, axis=-1)


def _odd_gates(xch, wax_ref, ba_ref, bx_ref, lam_ref):
    pre_r, pre_i = _odd_gate_dots(xch, wax_ref)
    tr = jnp.tanh(pre_r + ba_ref[...])
    ti = jnp.tanh(pre_i + bx_ref[...])
    lam = lam_ref[...]
    sp_neg_lam = jnp.maximum(-lam, 0.0) + jnp.log1p(jnp.exp(-jnp.abs(lam)))
    c2 = (-0.5 * LRU_C * LOG2E) * sp_neg_lam
    a = jnp.exp2(c2 * tr + c2)
    b = jnp.exp2(0.5 * jnp.log2(1.0 - a * a)) * xch * (ti + 1.0)
    return a, b


def _odd_prompt_kernel(x_ref, ln_ref, wu_ref, wg_ref, cw_ref, cb_ref, wax_ref, ba_ref, bx_ref, lam_ref,
                       wout_ref, y_ref, hl_ref, cv_ref, ush_scr, hc_scr, *, tt):
    tb = pl.program_id(1)
    ns = SUBLANES

    @pl.when(tb == 0)
    def _():
        ush_scr[0:ns, :] = jnp.zeros((ns, D_RNN), F32)
        hc_scr[...] = jnp.zeros_like(hc_scr)

    x = x_ref[0]
    xb = _rms(x, ln_ref[...]).astype(BF16)
    u = _dot(xb, wu_ref[...])
    gate = _dot(xb, wg_ref[...])
    ush_scr[ns:ns + tt, :] = u
    cw = cw_ref[...]
    xc = cb_ref[...] + u * cw[CONV_W - 1:CONV_W, :]
    for s in range(1, CONV_W):
        xc = xc + ush_scr[ns - s:ns - s + tt, :] * cw[CONV_W - 1 - s:CONV_W - s, :]
    ush_scr[0:ns, :] = ush_scr[tt:tt + ns, :]
    a, b = _odd_gates(xc, wax_ref, ba_ref, bx_ref, lam_ref)
    a, b = _group_scan(a, b)
    c = hc_scr[...]
    hs = []
    for g in range(tt // ns):
        hg = a[g * ns:(g + 1) * ns] * c + b[g * ns:(g + 1) * ns]
        hs.append(hg)
        c = hg[ns - 1:ns, :]
    hc_scr[...] = c
    h = jnp.concatenate(hs, axis=0)
    y_ref[0] = x + _dot((h * _half_silu(gate)).astype(BF16), wout_ref[...])

    @pl.when(tb == pl.num_programs(1) - 1)
    def _():
        hl_ref[0] = c
        cv_ref[0] = ush_scr[ns - (CONV_W - 1):ns, :]


def _odd_sample_kernel(x_ref, ext_ref, ln_ref, wu_ref, wg_ref, cw_ref, cb_ref, wax_ref, ba_ref, bx_ref, lam_ref,
                       wout_ref, y_ref, h_ref, u_ref):
    ns = SUBLANES
    x = x_ref[...]
    rows = x.shape[0]
    xb = _rms(x, ln_ref[...]).astype(BF16)
    u = _dot(xb, wu_ref[...])
    gate = _dot(xb, wg_ref[...])
    u_ref[...] = u
    ext = ext_ref[...]
    t = lax.broadcasted_iota(jnp.int32, (rows, D_RNN), 0) % ns
    cw = cw_ref[...]
    xc = cb_ref[...] + u * cw[CONV_W - 1:CONV_W, :]
    for s in range(1, CONV_W):
        prev = jnp.where(t < s, pltpu.roll(ext, rows + s - ns, 0), pltpu.roll(u, s, 0))
        xc = xc + prev * cw[CONV_W - 1 - s:CONV_W - s, :]
    a, b = _odd_gates(xc, wax_ref, ba_ref, bx_ref, lam_ref)
    h0_at0 = jnp.where(t == 0, pltpu.roll(ext, rows - (ns - CONV_W), 0), 0.0)
    _, h = _group_scan(a, b + a * h0_at0)
    h_ref[...] = h
    y_ref[...] = x + _dot((h * _half_silu(gate)).astype(BF16), wout_ref[...])


_ODD_WEIGHTS = ('ln', 'wu', 'wg', 'cw', 'cb', 'wax', 'ba', 'bx', 'lam', 'wout')


def _odd_prompt(x, w, tt):
    b, t, d = x.shape
    wl = [w[n] for n in _ODD_WEIGHTS]
    return pl.pallas_call(
        functools.partial(_odd_prompt_kernel, tt=tt),
        grid=(b, t // tt),
        in_specs=[pl.BlockSpec((1, tt, d), lambda i, j: (i, j, 0))] + [_resident(a.shape) for a in wl],
        out_specs=[pl.BlockSpec((1, tt, d), lambda i, j: (i, j, 0)),
                   pl.BlockSpec((1, 1, D_RNN), lambda i, j: (i, 0, 0)),
                   pl.BlockSpec((1, CONV_W - 1, D_RNN), lambda i, j: (i, 0, 0))],
        out_shape=[jax.ShapeDtypeStruct((b, t, d), F32), jax.ShapeDtypeStruct((b, 1, D_RNN), F32),
                   jax.ShapeDtypeStruct((b, CONV_W - 1, D_RNN), F32)],
        scratch_shapes=[pltpu.VMEM((tt + SUBLANES, D_RNN), F32), pltpu.VMEM((1, D_RNN), F32)],
        compiler_params=_params(("parallel", "arbitrary")),
        name="odd_prompt",
    )(x, *wl)


def _odd_sample(x2, ext, w, rows):
    n, d = x2.shape
    wl = [w[n_] for n_ in _ODD_WEIGHTS]
    row = lambda w_: pl.BlockSpec((rows, w_), lambda i: (i, 0))
    return pl.pallas_call(
        _odd_sample_kernel,
        grid=(n // rows,),
        in_specs=[row(d), row(D_RNN)] + [_resident(a.shape) for a in wl],
        out_specs=[row(d), row(D_RNN), row(D_RNN)],
        out_shape=[jax.ShapeDtypeStruct((n, d), F32), jax.ShapeDtypeStruct((n, D_RNN), F32),
                   jax.ShapeDtypeStruct((n, D_RNN), F32)],
        compiler_params=_params(("parallel",)),
        name="odd_sample",
    )(x2, ext, *wl)


def _pad_heads(v, n_used):
    lead = v.shape[:-1]
    v = v.reshape(lead + (MLA_HEADS, n_used))
    v = jnp.pad(v, [(0, 0)] * len(lead) + [(0, 0), (0, HEAD_PAD - n_used)])
    return v.reshape(lead + (MLA_PADW,))


def _rot_half(w):
    return jnp.concatenate([-w[..., HALF_ROPE:], w[..., :HALF_ROPE]], axis=-1)


def _swap_half_gain(g):
    return jnp.concatenate([g[:MLA_NOPE], g[MLA_NOPE + HALF_ROPE:], g[MLA_NOPE:MLA_NOPE + HALF_ROPE]])


def _pack_even(j, ln_even, w_in_even, gla_w_f2, gla_b_f, gla_out_norm, mla_q_norm, mla_kv_norm, mla_w_uq,
               mla_w_ukv, mla_qh_norm, mla_kh_norm, w_out_even):
    wi = w_in_even[j]
    d = wi.shape[0]
    o = 0
    seg = {}
    for name, width in (('qa', GLA_QK), ('ka', GLA_QK), ('va', GLA_VW), ('ga', GLA_VW), ('fa', GLA_RANK),
                        ('cq', Q_LORA), ('ckv', KV_LORA), ('kr', MLA_ROPE), ('gb', MLA_VW)):
        seg[name] = wi[:, o:o + width]
        o += width
    z = lambda n: jnp.zeros((d, n), wi.dtype)
    tail = MISC_W - ROPE_LANE0 - MLA_ROPE
    misc = jnp.concatenate([seg['fa'], z(ROPE_LANE0 - GLA_RANK), seg['kr'], z(tail)], 1)
    misc_rot = jnp.concatenate([z(ROPE_LANE0), _rot_half(seg['kr']), z(tail)], 1)
    win = jnp.concatenate([seg['qa'], seg['ka'], seg['va'], 0.5 * seg['ga'], seg['cq'], seg['ckv'],
                           0.5 * seg['gb'], misc, misc_rot], 1)
    wf2 = jnp.pad(gla_w_f2[j], ((0, MISC_W - GLA_RANK), (0, 0)))
    ukv = mla_w_ukv[j].reshape(KV_LORA, MLA_HEADS, MLA_NOPE + MLA_V)
    wuk = ukv[:, :, :MLA_NOPE].reshape(KV_LORA, MLA_NOPEW)
    wuv = ukv[:, :, MLA_NOPE:].reshape(KV_LORA, MLA_VW)
    uq = mla_w_uq[j].reshape(Q_LORA, MLA_HEADS, MLA_QK)
    uq_rot = jnp.concatenate([jnp.zeros_like(uq[..., :MLA_NOPE]), _rot_half(uq[..., MLA_NOPE:])], -1)
    khp = jnp.pad(mla_kh_norm[j], (0, HEAD_PAD - MLA_QK))
    return {
        'ln': ln_even[j][None, :],
        'win': win.astype(BF16),
        'wf2': wf2.astype(BF16),
        'bf': gla_b_f[j][None, :],
        'qn': mla_q_norm[j][None, :],
        'kvn': mla_kv_norm[j][None, :],
        'wuq': _pad_heads(mla_w_uq[j], MLA_QK).astype(BF16),
        'wuqr': _pad_heads(uq_rot.reshape(Q_LORA, MLA_HEADS * MLA_QK), MLA_QK).astype(BF16),
        'wukp': _pad_heads(wuk, MLA_NOPE).astype(BF16),
        'wukt': wuk.T.astype(BF16),
        'wuv': wuv.astype(BF16),
        'wuvt': wuv.T.astype(BF16),
        'qh': jnp.pad(mla_qh_norm[j], (0, HEAD_PAD - MLA_QK))[None, :],
        'qhs': jnp.pad(_swap_half_gain(mla_qh_norm[j]), (0, HEAD_PAD - MLA_QK))[None, :],
        'kh': khp[None, :],
        'khs': jnp.pad(_swap_half_gain(mla_kh_norm[j]), (0, HEAD_PAD - MLA_QK))[None, :],
        'khg': jnp.tile(khp, MLA_HEADS)[None, :],
        'grc': mla_kh_norm[j][MLA_NOPE:][:, None],
        'on': gla_out_norm[j][None, :],
        'wout_a': w_out_even[j][:GLA_VW].astype(BF16),
        'wout_b': w_out_even[j][GLA_VW:].astype(BF16),
    }


def _pack_odd(j, ln_odd, w_in_odd, conv_w, conv_b, rg_w_a, rg_b_a, rg_w_x, rg_b_x, rg_lambda, w_out_odd):
    return {
        'ln': ln_odd[j][None, :],
        'wu': w_in_odd[j][:, :D_RNN].astype(BF16),
        'wg': (0.5 * w_in_odd[j][:, D_RNN:]).astype(BF16),
        'cw': 0.5 * conv_w[j],
        'cb': 0.5 * conv_b[j][None, :],
        'wax': jnp.concatenate([rg_w_a[j], rg_w_x[j]], axis=-1).astype(BF16),
        'ba': 0.5 * rg_b_a[j][None, :],
        'bx': 0.5 * rg_b_x[j][None, :],
        'lam': rg_lambda[j][None, :],
        'wout': w_out_odd[j].astype(BF16),
    }


def _rope_angles(pos):
    inv_freq = ROPE_THETA ** (-jnp.arange(HALF_ROPE, dtype=F32) / HALF_ROPE)
    ang = pos.astype(F32)[:, None] * inv_freq[None, :]
    return jnp.cos(ang), jnp.sin(ang)


def _rope_tile_tables(pos):
    cos, sin = _rope_angles(pos)
    n = pos.shape[0]
    tail = LANES - ROPE_LANE0 - MLA_ROPE
    c = jnp.concatenate([jnp.ones((n, ROPE_LANE0), F32), cos, cos, jnp.ones((n, tail), F32)], 1)
    s = jnp.concatenate([jnp.zeros((n, ROPE_LANE0), F32), sin, sin, jnp.zeros((n, tail), F32)], 1)
    return c, s


def _rope_pair_tables_t(pos):
    cos, sin = _rope_angles(pos)
    return jnp.concatenate([cos, cos], 1).T, jnp.concatenate([sin, sin], 1).T


def _pick(n, prefs):
    for p in prefs:
        if n % p == 0:
            return p
    return n


def kernel(x_prompt, x_sample, state_gla, cache_ckv, cache_kr, state_rglru_h, state_rglru_conv, page_table,
           ln_even, w_in_even, gla_w_f2, gla_b_f, gla_out_norm, mla_q_norm, mla_kv_norm, mla_w_uq, mla_w_ukv,
           mla_qh_norm, mla_kh_norm, w_out_even, ln_odd, w_in_odd, conv_w, conv_b, rg_w_a, rg_b_a, rg_w_x,
           rg_b_x, rg_lambda, w_out_odd):
    bp, tp, d = x_prompt.shape
    bs, ts, _ = x_sample.shape
    n_pages = page_table.shape[1]
    past_len = n_pages * PAGE_SIZE
    depth = ln_even.shape[0] + ln_odd.shape[0]
    assert ts == SUBLANES, "sample group is handled as one 8-row group per sequence"
    assert MLA_NOPE == MLA_V

    tm_p = _pick(tp, (512, 256, 128, 64, 32, 16, 8))
    tm_s = _pick(bs * ts, (256, 128, 64, 32, 16, 8))
    tm_back = _pick(tp, (512, 256, 128, 64, 32, 16, 8))
    nseq_p = _pick(bp, (2, 1))
    nseq_s = _pick(bs, (8, 4, 2, 1))
    chunk_p = min(GLA_CHUNK, tp)
    tblk_p = _pick(tp, (256, 128, 64)) if tp >= GLA_CHUNK else tp
    tq = _pick(tp, (512, 256, 128))
    tt_odd = _pick(tp, (512, 256, 128, 64))
    n_pg = _pick(n_pages, (PAGED_KEYS_PER_STEP // PAGE_SIZE, 32, 16, 8, 4, 2, 1))

    tabs_p = _rope_tile_tables(jnp.arange(tp))
    pos_s = past_len + jnp.arange(ts)
    tabs_s = tuple(jnp.tile(a, (tm_s // ts, 1)) for a in _rope_tile_tables(pos_s))
    pair_new = _rope_pair_tables_t(pos_s)
    pair_past = _rope_pair_tables_t(jnp.arange(past_len))
    pt_flat = page_table.reshape(-1).astype(jnp.int32)
    cache_krt = jnp.swapaxes(cache_kr, -1, -2)

    yp = x_prompt.reshape(bp * tp, d)
    ys = x_sample.reshape(bs * ts, d)
    outs = {k: [] for k in ('gla_p', 'ckv_p', 'kr_p', 'rh_p', 'rc_p', 'gla_s', 'ckv_s', 'kr_s', 'rh_s', 'rc_s')}
    for layer in range(depth):
        j = layer // 2
        if layer % 2 == 0:
            w = _pack_even(j, ln_even, w_in_even, gla_w_f2, gla_b_f, gla_out_norm, mla_q_norm, mla_kv_norm,
                           mla_w_uq, mla_w_ukv, mla_qh_norm, mla_kh_norm, w_out_even)
            qa, ka, va, lf, ga, gb, q, k, v, ckvn, misc = _even_front(yp, w, tabs_p, tm_p, tp)
            r3 = lambda a: a.reshape(bp, tp, a.shape[-1])
            oa, st = _gla_scan(r3(qa), r3(ka), r3(va), r3(lf), None, chunk_p, tblk_p, nseq_p)
            ob = _flash_attn(r3(q), r3(k), v, tq, FLASH_HEADS_PER_STEP)
            yp = _even_back(yp, oa.reshape(bp * tp, GLA_VW), ga, ob.reshape(bp * tp, MLA_VW), gb, w, tm_back)
            outs['gla_p'].append(st.reshape(bp, GLA_HEADS, GLA_DK, GLA_DV))
            outs['ckv_p'].append(ckvn.reshape(bp, tp, KV_LORA))
            outs['kr_p'].append(misc[:, ROPE_LANE0:ROPE_LANE0 + MLA_ROPE].reshape(bp, tp, MLA_ROPE))
            qa, ka, va, lf, ga, gb, q, k, v, ckvn, misc = _even_front(ys, w, tabs_s, tm_s, ts)
            r3 = lambda a: a.reshape(bs, ts, a.shape[-1])
            s0t = state_gla[j].reshape(bs, GLA_HEADS // 2, 2 * GLA_DK, GLA_DV)
            oa, st = _gla_scan(r3(qa), r3(ka), r3(va), r3(lf), s0t, ts, ts, nseq_s)
            krs = misc[:, ROPE_LANE0:ROPE_LANE0 + MLA_ROPE].reshape(bs, ts, MLA_ROPE)
            q4 = q.reshape(bs, ts, MLA_HEADS, HEAD_PAD)
            qrope = jnp.swapaxes(q4[..., ROPE_LANE0:ROPE_LANE0 + MLA_ROPE], 1, 2)
            qrope = qrope.reshape(bs, MLA_HEADS * ts, MLA_ROPE)
            qrr = jnp.concatenate([qrope, qrope[..., HALF_ROPE:], -qrope[..., :HALF_ROPE]], axis=-1)
            ob = _paged_attn(j, pt_flat, n_pages, r3(q), qrr, r3(ckvn), jnp.swapaxes(krs, 1, 2), pair_new,
                             pair_past, w, cache_ckv, cache_krt, n_pg)
            ys = _even_back(ys, oa.reshape(bs * ts, GLA_VW), ga, ob.reshape(bs * ts, MLA_VW), gb, w, tm_s)
            outs['gla_s'].append(st.reshape(bs, GLA_HEADS, GLA_DK, GLA_DV))
            outs['ckv_s'].append(ckvn.reshape(bs, ts, KV_LORA))
            outs['kr_s'].append(krs)
        else:
            w = _pack_odd(j, ln_odd, w_in_odd, conv_w, conv_b, rg_w_a, rg_b_a, rg_w_x, rg_b_x, rg_lambda,
                          w_out_odd)
            y3, hl, cv = _odd_prompt(yp.reshape(bp, tp, d), w, tt_odd)
            yp = y3.reshape(bp * tp, d)
            outs['rh_p'].append(hl.reshape(bp, D_RNN))
            outs['rc_p'].append(cv)
            ext = jnp.concatenate([jnp.zeros((bs, ts - CONV_W, D_RNN), F32), state_rglru_h[j][:, None, :],
                                   state_rglru_conv[j]], axis=1).reshape(bs * ts, D_RNN)
            ys, hfull, ufull = _odd_sample(ys, ext, w, tm_s)
            outs['rh_s'].append(hfull.reshape(bs, ts, D_RNN)[:, ts - 1])
            outs['rc_s'].append(ufull.reshape(bs, ts, D_RNN)[:, ts - (CONV_W - 1):])
    st_ = lambda name: jnp.stack(outs[name])
    return (yp.reshape(bp, tp, d), ys.reshape(bs, ts, d),
            st_('gla_p'), st_('ckv_p'), st_('kr_p'), st_('rh_p'), st_('rc_p'),
            st_('gla_s'), st_('ckv_s'), st_('kr_s'), st_('rh_s'), st_('rc_s'))
```

```python
import functools

import jax
import jax.numpy as jnp
from jax import lax
from jax.experimental import pallas as pl
from jax.experimental.pallas import tpu as pltpu

F32 = jnp.float32
BF16 = jnp.bfloat16
EPS = 1e-6

GLA_HEADS = 4
GLA_DK = 64
GLA_DV = 128
GLA_RANK = 16
GLA_TAU = 16.0
GLA_CHUNK = 64
GLA_QK = GLA_HEADS * GLA_DK
GLA_VW = GLA_HEADS * GLA_DV
MLA_HEADS = 8
MLA_NOPE = 64
MLA_ROPE = 32
MLA_V = 64
MLA_QK = MLA_NOPE + MLA_ROPE
MLA_VW = MLA_HEADS * MLA_V
MLA_NOPEW = MLA_HEADS * MLA_NOPE
Q_LORA = 384
KV_LORA = 256
ROPE_THETA = 10000.0
PAGE_SIZE = 128
D_RNN = 1280
RNN_BLOCKS = 10
RNN_BW = D_RNN // RNN_BLOCKS
CONV_W = 4
LRU_C = 8.0

LANES = 128
SUBLANES = 8
HEAD_PAD = LANES
MLA_PADW = MLA_HEADS * HEAD_PAD
HALF_ROPE = MLA_ROPE // 2
ROPE_LANE0 = MLA_NOPE
MISC_W = LANES
LOG2E = 1.4426950408889634
Q_SCALE = MLA_QK ** -0.5 * LOG2E
FLASH_HEADS_PER_STEP = 8
FLASH_TK = 512
EXP_CLAMP = 80.0
NEG_BIG = -1e30
VMEM_LIMIT = 56 * 1024 * 1024
PAGED_KEYS_PER_STEP = 8192

NT_DIMS = (((1,), (1,)), ((), ()))
TN_DIMS = (((0,), (0,)), ((), ()))


def _dot(a, b):
    return jnp.dot(a, b, preferred_element_type=F32)


def _dot_nt(a, b):
    return lax.dot_general(a, b, NT_DIMS, preferred_element_type=F32)


def _dot_tn(a, b):
    return lax.dot_general(a, b, TN_DIMS, preferred_element_type=F32)


def _rms(x, g, n=None):
    n = x.shape[-1] if n is None else n
    ss = jnp.sum(x * x, axis=-1, keepdims=True) * (1.0 / n)
    return x * lax.rsqrt(ss + EPS) * g


def _half_silu(gh):
    return gh * jnp.tanh(gh) + gh


def _log_sigmoid(x):
    return jnp.minimum(x, 0.0) - jnp.log1p(jnp.exp(-jnp.abs(x)))


def _params(sem):
    return pltpu.CompilerParams(dimension_semantics=sem, vmem_limit_bytes=VMEM_LIMIT)


def _full(shape):
    nd = len(shape)
    return pl.BlockSpec(shape, lambda *_: (0,) * nd)


def _resident(shape):
    nd = len(shape)
    return pl.BlockSpec(shape, lambda *_: (0,) * nd, pipeline_mode=pl.Buffered(1))


def _even_front_kernel(x_ref, ln_ref, win_ref, wf2_ref, bf_ref, qn_ref, kvn_ref, wuq_ref, wuqr_ref, wuk_ref,
                       wuv_ref, qh_ref, qhs_ref, kh_ref, khs_ref, c_ref, s_ref,
                       qa_ref, ka_ref, va_ref, lf_ref, ga_ref, gb_ref, q_ref, k_ref, v_ref,
                       ckv_ref, misc_ref):
    x = x_ref[...]
    xn = _rms(x, ln_ref[...])
    z = _dot(xn.astype(BF16), win_ref[...])
    o = 0
    qa_ref[...] = z[:, o:o + GLA_QK] * (GLA_DK ** -0.5); o += GLA_QK
    ka_ref[...] = z[:, o:o + GLA_QK]; o += GLA_QK
    va_ref[...] = z[:, o:o + GLA_VW]; o += GLA_VW
    ga_ref[...] = _half_silu(z[:, o:o + GLA_VW]); o += GLA_VW
    cq = z[:, o:o + Q_LORA]; o += Q_LORA
    ckv = z[:, o:o + KV_LORA]; o += KV_LORA
    gb_ref[...] = _half_silu(z[:, o:o + MLA_VW]); o += MLA_VW
    misc = z[:, o:o + MISC_W]; o += MISC_W
    misc_rot = z[:, o:o + MISC_W]
    misc_ref[...] = misc

    f = _dot(misc.astype(BF16), wf2_ref[...]) + bf_ref[...]
    lf_ref[...] = _log_sigmoid(f) * (1.0 / GLA_TAU)

    c = c_ref[...]
    s = s_ref[...]

    cqb = _rms(cq, qn_ref[...]).astype(BF16)
    ckvn = _rms(ckv, kvn_ref[...])
    ckv_ref[...] = ckvn
    cb = ckvn.astype(BF16)
    qraw = _dot(cqb, wuq_ref[...])
    qrot = _dot(cqb, wuqr_ref[...])
    kraw = _dot(cb, wuk_ref[...])
    v_ref[...] = _dot_nt(wuv_ref[...], cb).astype(BF16)
    gc = qh_ref[...] * c * Q_SCALE
    gs = qhs_ref[...] * s * Q_SCALE
    for h in range(MLA_HEADS):
        hs = slice(h * HEAD_PAD, (h + 1) * HEAD_PAD)
        t = qraw[:, hs]
        r = lax.rsqrt(jnp.sum(t * t, axis=-1, keepdims=True) * (1.0 / MLA_QK) + EPS)
        q_ref[:, hs] = ((t * gc + qrot[:, hs] * gs) * r).astype(BF16)

    lane = lax.broadcasted_iota(jnp.int32, misc.shape, 1)
    krt = jnp.where(lane >= ROPE_LANE0, misc, 0.0)
    gc = kh_ref[...] * c
    rot_gs = misc_rot * (khs_ref[...] * s)
    for h in range(MLA_HEADS):
        hs = slice(h * HEAD_PAD, (h + 1) * HEAD_PAD)
        t = kraw[:, hs] + krt
        r = lax.rsqrt(jnp.sum(t * t, axis=-1, keepdims=True) * (1.0 / MLA_QK) + EPS)
        k_ref[:, hs] = ((t * gc + rot_gs) * r).astype(BF16)


def _even_front(x2, w, tabs, tm, rows_per_seq):
    n, d = x2.shape
    nblk_per_seq = rows_per_seq // tm if rows_per_seq >= tm else None
    row = lambda w_: pl.BlockSpec((tm, w_), lambda i: (i, 0))
    if nblk_per_seq is not None:
        tab = pl.BlockSpec((tm, LANES), lambda i: (i % nblk_per_seq, 0))
    else:
        tab = pl.BlockSpec((tm, LANES), lambda i: (0, 0))
    ins = [x2, w['ln'], w['win'], w['wf2'], w['bf'], w['qn'], w['kvn'], w['wuq'], w['wuqr'], w['wukp'],
           w['wuvt'], w['qh'], w['qhs'], w['kh'], w['khs'], tabs[0], tabs[1]]
    in_specs = [row(d)] + [_resident(a.shape) for a in ins[1:15]] + [tab, tab]
    outs = [(GLA_QK, F32), (GLA_QK, F32), (GLA_VW, F32), (GLA_QK, F32), (GLA_VW, F32), (MLA_VW, F32),
            (MLA_PADW, BF16), (MLA_PADW, BF16), None, (KV_LORA, F32), (MISC_W, F32)]
    vt_spec = pl.BlockSpec((MLA_VW, tm), lambda i: (0, i))
    vt_shape = jax.ShapeDtypeStruct((MLA_VW, n), BF16)
    return pl.pallas_call(
        _even_front_kernel,
        grid=(n // tm,),
        in_specs=in_specs,
        out_specs=[vt_spec if o_ is None else row(o_[0]) for o_ in outs],
        out_shape=[vt_shape if o_ is None else jax.ShapeDtypeStruct((n, o_[0]), o_[1]) for o_ in outs],
        compiler_params=_params(("parallel",)),
        name="even_front",
    )(*ins)


def _gla_kernel(*refs, chunk, n_chunks, n_seq, has_s0):
    if has_s0:
        q_ref, k_ref, v_ref, g_ref, s0_ref, o_ref, st_ref, s_scr = refs
    else:
        q_ref, k_ref, v_ref, g_ref, o_ref, st_ref, s_scr = refs
    tb = pl.program_id(1)

    @pl.when(tb == 0)
    def _():
        if has_s0:
            s_scr[...] = s0_ref[...]
        else:
            s_scr[...] = jnp.zeros_like(s_scr)

    c_ = chunk
    row = lax.broadcasted_iota(jnp.int32, (c_, c_), 0)
    col = lax.broadcasted_iota(jnp.int32, (c_, c_), 1)
    tri = row >= col
    tri_bf = jnp.where(tri, 1.0, 0.0).astype(BF16)
    mid_row = c_ // 2 - 1
    lane_head = lax.broadcasted_iota(jnp.int32, (c_, LANES), 1) // GLA_DK
    zero_bf = jnp.zeros((c_, LANES), BF16)
    for c in range(n_chunks):
        sl = slice(c * c_, (c + 1) * c_)
        cums = []
        for b_ in range(n_seq):
            g = g_ref[b_, sl, :]
            g_hi = g.astype(BF16)
            g_lo = (g - g_hi.astype(F32)).astype(BF16)
            cums.append(_dot(tri_bf, g_hi) + _dot(tri_bf, g_lo))
        work = []
        for b_ in range(n_seq):
            cum = cums[b_]
            tot = cum[c_ - 1:c_, :]
            mid = cum[mid_row:mid_row + 1, :]
            q = q_ref[b_, sl, :]
            k = k_ref[b_, sl, :]
            qd = (q * jnp.exp(cum)).astype(BF16)
            qm = (q * jnp.exp(jnp.minimum(cum - mid, EXP_CLAMP))).astype(BF16)
            km = (k * jnp.exp(jnp.minimum(mid - cum, EXP_CLAMP))).astype(BF16)
            kd = (k * jnp.exp(tot - cum)).astype(BF16)
            etot = jnp.exp(tot)
            for pr in range(GLA_HEADS // 2):
                ts = slice(pr * LANES, (pr + 1) * LANES)
                heads = []
                for hh in range(2):
                    h = 2 * pr + hh
                    mine = lane_head == hh
                    vs = slice(h * GLA_DV, (h + 1) * GLA_DV)
                    vh = v_ref[b_, sl, vs].astype(BF16)
                    a = _dot_nt(jnp.where(mine, qm[:, ts], zero_bf), km[:, ts])
                    heads.append((vs, vh, a, jnp.where(mine, qd[:, ts], zero_bf),
                                  jnp.where(mine, kd[:, ts], zero_bf)))
                work.append((b_, pr, etot[:, ts], heads))
        upds = [_dot_tn(jnp.concatenate([hd[4] for hd in heads], axis=0),
                        jnp.concatenate([hd[1] for hd in heads], axis=0)) for _, _, _, heads in work]
        for (b_, pr, etot_t, heads), upd in zip(work, upds):
            st = s_scr[b_, pr]
            st_bf = st.astype(BF16)
            for vs, vh, a, qdm, _ in heads:
                a = jnp.where(tri, a, 0.0).astype(BF16)
                if c_ % GLA_DK == 0:
                    o_ref[b_, sl, vs] = _dot(jnp.concatenate([qdm, a], axis=1),
                                             jnp.concatenate([st_bf, vh], axis=0))
                else:
                    o_ref[b_, sl, vs] = _dot(qdm, st_bf) + _dot(a, vh)
            decay = jnp.transpose(jnp.broadcast_to(etot_t, (LANES, LANES)))
            s_scr[b_, pr] = st * decay + upd

    @pl.when(tb == pl.num_programs(1) - 1)
    def _():
        st_ref[...] = s_scr[...]


def _gla_scan(qa, ka, va, lf, s0t, chunk, tblk, n_seq):
    b, t, _ = qa.shape
    n_chunks = tblk // chunk
    has_s0 = s0t is not None
    seq = lambda w_: pl.BlockSpec((n_seq, tblk, w_), lambda i, j: (i, j, 0))
    st_spec = pl.BlockSpec((n_seq, GLA_HEADS // 2, 2 * GLA_DK, GLA_DV), lambda i, j: (i, 0, 0, 0))
    ins = [qa, ka, va, lf] + ([s0t] if has_s0 else [])
    in_specs = [seq(GLA_QK), seq(GLA_QK), seq(GLA_VW), seq(GLA_QK)] + ([st_spec] if has_s0 else [])
    return pl.pallas_call(
        functools.partial(_gla_kernel, chunk=chunk, n_chunks=n_chunks, n_seq=n_seq, has_s0=has_s0),
        grid=(b // n_seq, t // tblk),
        in_specs=in_specs,
        out_specs=[seq(GLA_VW), st_spec],
        out_shape=[jax.ShapeDtypeStruct((b, t, GLA_VW), F32),
                   jax.ShapeDtypeStruct((b, GLA_HEADS // 2, 2 * GLA_DK, GLA_DV), F32)],
        scratch_shapes=[pltpu.VMEM((n_seq, GLA_HEADS // 2, 2 * GLA_DK, GLA_DV), F32)],
        compiler_params=_params(("parallel", "arbitrary")),
        name="gla_scan",
    )(*ins)


def _flash_kernel(q_ref, k_ref, vt_ref, o_ref, m_scr, l_scr, acc_scr, *, tq, tk, n_heads):
    qi = pl.program_id(2)
    m_scr[...] = jnp.full_like(m_scr, NEG_BIG)
    l_scr[...] = jnp.zeros_like(l_scr)
    acc_scr[...] = jnp.zeros_like(acc_scr)

    def block(ki, diagonal):
        ks = pl.ds(pl.multiple_of(ki * tk, tk), tk)
        sts = []
        for h in range(n_heads):
            kh = k_ref[0, ks, h * HEAD_PAD:(h + 1) * HEAD_PAD]
            qh = q_ref[0, :, h * HEAD_PAD:(h + 1) * HEAD_PAD]
            sts.append(_dot_nt(kh, qh))
        for h in range(n_heads):
            st = sts[h]
            if diagonal:
                key = lax.broadcasted_iota(jnp.int32, (tk, tq), 0) + ki * tk
                qry = lax.broadcasted_iota(jnp.int32, (tk, tq), 1) + qi * tq
                st = jnp.where(key <= qry, st, NEG_BIG)
            m_prev = m_scr[h]
            m_new = jnp.maximum(m_prev, jnp.max(st, axis=0, keepdims=True))
            alpha = jnp.exp2(m_prev - m_new)
            p = jnp.exp2(st - m_new)
            l_scr[h] = alpha * l_scr[h] + jnp.sum(p, axis=0, keepdims=True)
            m_scr[h] = m_new
            vt = vt_ref[h * MLA_V:(h + 1) * MLA_V, ks]
            acc_scr[h] = alpha * acc_scr[h] + _dot(vt, p.astype(BF16))

    def body(ki, carry):
        block(ki, False)
        return carry

    per_q = tq // tk
    lax.fori_loop(0, qi * per_q, body, 0)
    for j in range(per_q):
        block(qi * per_q + j, True)
    for pr in range(n_heads // 2):
        ot = jnp.concatenate([acc_scr[2 * pr] / l_scr[2 * pr], acc_scr[2 * pr + 1] / l_scr[2 * pr + 1]], axis=0)
        o_ref[0, :, pr * LANES:(pr + 1) * LANES] = jnp.transpose(ot)


def _flash_attn(q, k, vt, tq, n_heads):
    b, t, _ = q.shape
    nq = t // tq
    ng = MLA_HEADS // n_heads
    return pl.pallas_call(
        functools.partial(_flash_kernel, tq=tq, tk=min(FLASH_TK, tq), n_heads=n_heads),
        grid=(b, ng, nq),
        in_specs=[pl.BlockSpec((1, tq, n_heads * HEAD_PAD), lambda bi, g, i: (bi, i, g)),
                  pl.BlockSpec((1, t, n_heads * HEAD_PAD), lambda bi, g, i: (bi, 0, g)),
                  pl.BlockSpec((n_heads * MLA_V, t), lambda bi, g, i: (g, bi))],
        out_specs=pl.BlockSpec((1, tq, n_heads * MLA_V), lambda bi, g, i: (bi, i, g)),
        out_shape=jax.ShapeDtypeStruct((b, t, MLA_VW), F32),
        scratch_shapes=[pltpu.VMEM((n_heads, 1, tq), F32), pltpu.VMEM((n_heads, 1, tq), F32),
                        pltpu.VMEM((n_heads, MLA_V, tq), F32)],
        compiler_params=_params(("parallel", "parallel", "arbitrary")),
        name="flash_attn",
    )(q, k, vt)


def _paged_kernel(pt_ref, q_ref, qrr_ref, latn_ref, krnt_ref, cosn_ref, sinn_ref, cos_ref, sin_ref,
                  khg_ref, grc_ref, wukp_ref, wukt_ref, wuv_ref, ckv_hbm, krt_hbm,
                  o_ref, lat_buf, krt_buf, sem, lhs_scr, m_scr, l_scr, acc_scr, *, layer, n_pg, t_new):
    p_idx = pl.program_id(1)
    steps = pl.num_programs(1)
    step = pl.program_id(0) * steps + p_idx
    total = pl.num_programs(0) * steps
    slot = lax.rem(step, 2)
    nq = MLA_HEADS * t_new
    tt = n_pg * PAGE_SIZE

    def page_copies(st, sl):
        cps = []
        for g in range(n_pg):
            page = pt_ref[st * n_pg + g]
            rows = pl.ds(g * PAGE_SIZE, PAGE_SIZE)
            cps.append(pltpu.make_async_copy(ckv_hbm.at[layer, page], lat_buf.at[sl, rows, :], sem.at[sl, 0]))
            cps.append(pltpu.make_async_copy(krt_hbm.at[layer, page], krt_buf.at[sl, :, rows], sem.at[sl, 1]))
        return cps

    @pl.when(step == 0)
    def _():
        for cp in page_copies(0, 0):
            cp.start()

    @pl.when(step + 1 < total)
    def _():
        for cp in page_copies(step + 1, 1 - slot):
            cp.start()

    for cp in page_copies(step, slot):
        cp.wait()

    @pl.when(p_idx == 0)
    def _():
        qt = q_ref[0].astype(F32)
        qrep = jnp.concatenate([qt] * MLA_HEADS, axis=0)
        r_ = lax.broadcasted_iota(jnp.int32, qrep.shape, 0) // t_new
        c_ = lax.broadcasted_iota(jnp.int32, qrep.shape, 1) // HEAD_PAD
        qbd = jnp.where(r_ == c_, qrep * khg_ref[...], 0.0)
        lhs_scr[0:nq, :] = _dot_nt(qbd.astype(BF16), wukp_ref[...]).astype(BF16)
        lhs_scr[nq:, :] = wukt_ref[...]
        m_scr[...] = jnp.full_like(m_scr, NEG_BIG)
        l_scr[...] = jnp.zeros_like(l_scr)
        acc_scr[...] = jnp.zeros_like(acc_scr)

    grc = grc_ref[...]
    qrr = qrr_ref[0]

    def attend(cb, kt, cost, sint, mask):
        n = cb.shape[0]
        x = kt * grc
        xx = jnp.concatenate([x * cost, x * sint], axis=0).astype(BF16)
        big = _dot_nt(lhs_scr[...], cb)
        rope = _dot(qrr, xx)
        k2 = jnp.sum(kt * kt, axis=0, keepdims=True)
        rows = []
        for h in range(MLA_HEADS):
            blk = big[nq + h * MLA_NOPE:nq + (h + 1) * MLA_NOPE]
            ssq = jnp.sum(blk * blk, axis=0, keepdims=True) + k2
            rows.append(jnp.broadcast_to(lax.rsqrt(ssq * (1.0 / MLA_QK) + EPS), (t_new, n)))
        s = (big[0:nq] + rope) * jnp.concatenate(rows, axis=0)
        if mask is not None:
            s = jnp.where(mask, s, NEG_BIG)
        m_prev = m_scr[...]
        m_new = jnp.maximum(m_prev, jnp.max(s, axis=-1, keepdims=True))
        alpha = jnp.exp2(m_prev - m_new)
        p = jnp.exp2(s - m_new[:, 0:1])
        l_scr[...] = alpha * l_scr[...] + jnp.sum(p, axis=-1, keepdims=True)
        m_scr[...] = m_new
        acc_scr[...] = alpha[:, 0:1] * acc_scr[...] + _dot(p.astype(BF16), cb)

    ks = pl.ds(pl.multiple_of(p_idx * tt, tt), tt)
    attend(lat_buf[slot].astype(BF16), krt_buf[slot], cos_ref[:, ks], sin_ref[:, ks], None)

    @pl.when(p_idx == pl.num_programs(1) - 1)
    def _():
        qi = lax.broadcasted_iota(jnp.int32, (nq, t_new), 0) % t_new
        kj = lax.broadcasted_iota(jnp.int32, (nq, t_new), 1)
        attend(latn_ref[0].astype(BF16), krnt_ref[0], cosn_ref[...], sinn_ref[...], kj <= qi)
        out_lat = acc_scr[...] / l_scr[:, 0:1]
        er = lax.broadcasted_iota(jnp.int32, (nq, MLA_VW), 0) // t_new
        ec = lax.broadcasted_iota(jnp.int32, (nq, MLA_VW), 1) // MLA_V
        full = jnp.where(er == ec, _dot(out_lat.astype(BF16), wuv_ref[...]), 0.0)
        o = full[0:t_new]
        for h in range(1, MLA_HEADS):
            o = o + full[h * t_new:(h + 1) * t_new]
        o_ref[0] = o


def _paged_attn(layer, page_table_flat, n_pages, q3, qrr, latn, krnt, tabs_new, tabs_past, w,
                cache_ckv, cache_krt, n_pg):
    b, t_new, _ = q3.shape
    nq = MLA_HEADS * t_new
    steps = n_pages // n_pg
    tt = n_pg * PAGE_SIZE

    def per_b(shape):
        nd = len(shape)
        return pl.BlockSpec((1,) + shape[1:], lambda bi, p, pt: (bi,) + (0,) * (nd - 1))

    def const(shape):
        nd = len(shape)
        return pl.BlockSpec(shape, lambda bi, p, pt: (0,) * nd, pipeline_mode=pl.Buffered(1))

    hbm = pl.BlockSpec(memory_space=pl.ANY)
    ins = [q3, qrr, latn, krnt, tabs_new[0], tabs_new[1], tabs_past[0], tabs_past[1],
           w['khg'], w['grc'], w['wukp'], w['wukt'], w['wuv'], cache_ckv, cache_krt]
    in_specs = [per_b(a.shape) for a in ins[:4]] + [const(a.shape) for a in ins[4:13]] + [hbm, hbm]
    grid_spec = pltpu.PrefetchScalarGridSpec(
        num_scalar_prefetch=1,
        grid=(b, steps),
        in_specs=in_specs,
        out_specs=pl.BlockSpec((1, t_new, MLA_VW), lambda bi, p, pt: (bi, 0, 0)),
        scratch_shapes=[pltpu.VMEM((2, tt, KV_LORA), F32), pltpu.VMEM((2, MLA_ROPE, tt), F32),
                        pltpu.SemaphoreType.DMA((2, 2)),
                        pltpu.VMEM((nq + MLA_NOPEW, KV_LORA), BF16), pltpu.VMEM((nq, LANES), F32),
                        pltpu.VMEM((nq, LANES), F32), pltpu.VMEM((nq, KV_LORA), F32)],
    )
    return pl.pallas_call(
        functools.partial(_paged_kernel, layer=layer, n_pg=n_pg, t_new=t_new),
        grid_spec=grid_spec,
        out_shape=jax.ShapeDtypeStruct((b, t_new, MLA_VW), F32),
        compiler_params=_params(("arbitrary", "arbitrary")),
        name="paged_attn",
    )(page_table_flat, *ins)


def _even_back_kernel(x_ref, oa_ref, ga_ref, ob_ref, gb_ref, on_ref, wa_ref, wb_ref, y_ref):
    oa = oa_ref[...]
    on = on_ref[...]
    parts = []
    for h in range(GLA_HEADS):
        parts.append(_rms(oa[:, h * GLA_DV:(h + 1) * GLA_DV], on))
    a = jnp.concatenate(parts, axis=-1) * ga_ref[...]
    b_ = ob_ref[...] * gb_ref[...]
    y_ref[...] = x_ref[...] + _dot(a.astype(BF16), wa_ref[...]) + _dot(b_.astype(BF16), wb_ref[...])


def _even_back(x2, oa, ga, ob, gb, w, tm):
    n, d = x2.shape
    row = lambda w_: pl.BlockSpec((tm, w_), lambda i: (i, 0))
    ins = [x2, oa, ga, ob, gb, w['on'], w['wout_a'], w['wout_b']]
    return pl.pallas_call(
        _even_back_kernel,
        grid=(n // tm,),
        in_specs=[row(d), row(GLA_VW), row(GLA_VW), row(MLA_VW), row(MLA_VW)] + [_full(a.shape) for a in ins[5:]],
        out_specs=row(d),
        out_shape=jax.ShapeDtypeStruct((n, d), F32),
        compiler_params=_params(("parallel",)),
        name="even_back",
    )(*ins)


def _group_scan(a, b):
    rows, width = a.shape
    a = a.reshape(rows // SUBLANES, SUBLANES, width)
    b = b.reshape(rows // SUBLANES, SUBLANES, width)
    t = lax.broadcasted_iota(jnp.int32, a.shape, 1)
    for s in (1, 2, 4):
        keep = t >= s
        ar = pltpu.roll(a, s, 1)
        br = pltpu.roll(b, s, 1)
        b = jnp.where(keep, a * br + b, b)
        a = jnp.where(keep, a * ar, a)
    return a.reshape(rows, width), b.reshape(rows, width)


def _odd_gate_dots(xc, wax_ref):
    xcb = xc.astype(BF16)
    rs, is_ = [], []
    for n in range(RNN_BLOCKS):
        ga = _dot(xcb[:, n * RNN_BW:(n + 1) * RNN_BW], wax_ref[n])
        rs.append(ga[:, :RNN_BW])
        is_.append(ga[:, RNN_BW:])
    return jnp.concatenate(rs, axis=-1), jnp.concatenate(is_, axis=-1)


def _odd_gates(xch, wax_ref, ba_ref, bx_ref, lam_ref):
    pre_r, pre_i = _odd_gate_dots(xch, wax_ref)
    tr = jnp.tanh(pre_r + ba_ref[...])
    ti = jnp.tanh(pre_i + bx_ref[...])
    lam = lam_ref[...]
    sp_neg_lam = jnp.maximum(-lam, 0.0) + jnp.log1p(jnp.exp(-jnp.abs(lam)))
    c2 = (-0.5 * LRU_C * LOG2E) * sp_neg_lam
    a = jnp.exp2(c2 * tr + c2)
    b = jnp.exp2(0.5 * jnp.log2(1.0 - a * a)) * xch * (ti + 1.0)
    return a, b


def _odd_prompt_kernel(x_ref, ln_ref, wu_ref, wg_ref, cw_ref, cb_ref, wax_ref, ba_ref, bx_ref, lam_ref,
                       wout_ref, y_ref, hl_ref, cv_ref, ush_scr, hc_scr, *, tt):
    tb = pl.program_id(1)
    ns = SUBLANES

    @pl.when(tb == 0)
    def _():
        ush_scr[0:ns, :] = jnp.zeros((ns, D_RNN), F32)
        hc_scr[...] = jnp.zeros_like(hc_scr)

    x = x_ref[0]
    xb = _rms(x, ln_ref[...]).astype(BF16)
    u = _dot(xb, wu_ref[...])
    gate = _dot(xb, wg_ref[...])
    ush_scr[ns:ns + tt, :] = u
    cw = cw_ref[...]
    xc = cb_ref[...] + u * cw[CONV_W - 1:CONV_W, :]
    for s in range(1, CONV_W):
        xc = xc + ush_scr[ns - s:ns - s + tt, :] * cw[CONV_W - 1 - s:CONV_W - s, :]
    ush_scr[0:ns, :] = ush_scr[tt:tt + ns, :]
    a, b = _odd_gates(xc, wax_ref, ba_ref, bx_ref, lam_ref)
    a, b = _group_scan(a, b)
    c = hc_scr[...]
    hs = []
    for g in range(tt // ns):
        hg = a[g * ns:(g + 1) * ns] * c + b[g * ns:(g + 1) * ns]
        hs.append(hg)
        c = hg[ns - 1:ns, :]
    hc_scr[...] = c
    h = jnp.concatenate(hs, axis=0)
    y_ref[0] = x + _dot((h * _half_silu(gate)).astype(BF16), wout_ref[...])

    @pl.when(tb == pl.num_programs(1) - 1)
    def _():
        hl_ref[0] = c
        cv_ref[0] = ush_scr[ns - (CONV_W - 1):ns, :]


def _odd_sample_kernel(x_ref, ext_ref, ln_ref, wu_ref, wg_ref, cw_ref, cb_ref, wax_ref, ba_ref, bx_ref, lam_ref,
                       wout_ref, y_ref, h_ref, u_ref):
    ns = SUBLANES
    x = x_ref[...]
    rows = x.shape[0]
    xb = _rms(x, ln_ref[...]).astype(BF16)
    u = _dot(xb, wu_ref[...])
    gate = _dot(xb, wg_ref[...])
    u_ref[...] = u
    ext = ext_ref[...]
    t = lax.broadcasted_iota(jnp.int32, (rows, D_RNN), 0) % ns
    cw = cw_ref[...]
    xc = cb_ref[...] + u * cw[CONV_W - 1:CONV_W, :]
    for s in range(1, CONV_W):
        prev = jnp.where(t < s, pltpu.roll(ext, rows + s - ns, 0), pltpu.roll(u, s, 0))
        xc = xc + prev * cw[CONV_W - 1 - s:CONV_W - s, :]
    a, b = _odd_gates(xc, wax_ref, ba_ref, bx_ref, lam_ref)
    h0_at0 = jnp.where(t == 0, pltpu.roll(ext, rows - (ns - CONV_W), 0), 0.0)
    _, h = _group_scan(a, b + a * h0_at0)
    h_ref[...] = h
    y_ref[...] = x + _dot((h * _half_silu(gate)).astype(BF16), wout_ref[...])


_ODD_WEIGHTS = ('ln', 'wu', 'wg', 'cw', 'cb', 'wax', 'ba', 'bx', 'lam', 'wout')


def _odd_prompt(x, w, tt):
    b, t, d = x.shape
    wl = [w[n] for n in _ODD_WEIGHTS]
    return pl.pallas_call(
        functools.partial(_odd_prompt_kernel, tt=tt),
        grid=(b, t // tt),
        in_specs=[pl.BlockSpec((1, tt, d), lambda i, j: (i, j, 0))] + [_resident(a.shape) for a in wl],
        out_specs=[pl.BlockSpec((1, tt, d), lambda i, j: (i, j, 0)),
                   pl.BlockSpec((1, 1, D_RNN), lambda i, j: (i, 0, 0)),
                   pl.BlockSpec((1, CONV_W - 1, D_RNN), lambda i, j: (i, 0, 0))],
        out_shape=[jax.ShapeDtypeStruct((b, t, d), F32), jax.ShapeDtypeStruct((b, 1, D_RNN), F32),
                   jax.ShapeDtypeStruct((b, CONV_W - 1, D_RNN), F32)],
        scratch_shapes=[pltpu.VMEM((tt + SUBLANES, D_RNN), F32), pltpu.VMEM((1, D_RNN), F32)],
        compiler_params=_params(("parallel", "arbitrary")),
        name="odd_prompt",
    )(x, *wl)


def _odd_sample(x2, ext, w, rows):
    n, d = x2.shape
    wl = [w[n_] for n_ in _ODD_WEIGHTS]
    row = lambda w_: pl.BlockSpec((rows, w_), lambda i: (i, 0))
    return pl.pallas_call(
        _odd_sample_kernel,
        grid=(n // rows,),
        in_specs=[row(d), row(D_RNN)] + [_resident(a.shape) for a in wl],
        out_specs=[row(d), row(D_RNN), row(D_RNN)],
        out_shape=[jax.ShapeDtypeStruct((n, d), F32), jax.ShapeDtypeStruct((n, D_RNN), F32),
                   jax.ShapeDtypeStruct((n, D_RNN), F32)],
        compiler_params=_params(("parallel",)),
        name="odd_sample",
    )(x2, ext, *wl)


def _pad_heads(v, n_used):
    lead = v.shape[:-1]
    v = v.reshape(lead + (MLA_HEADS, n_used))
    v = jnp.pad(v, [(0, 0)] * len(lead) + [(0, 0), (0, HEAD_PAD - n_used)])
    return v.reshape(lead + (MLA_PADW,))


def _rot_half(w):
    return jnp.concatenate([-w[..., HALF_ROPE:], w[..., :HALF_ROPE]], axis=-1)


def _swap_half_gain(g):
    return jnp.concatenate([g[:MLA_NOPE], g[MLA_NOPE + HALF_ROPE:], g[MLA_NOPE:MLA_NOPE + HALF_ROPE]])


def _pack_even(j, ln_even, w_in_even, gla_w_f2, gla_b_f, gla_out_norm, mla_q_norm, mla_kv_norm, mla_w_uq,
               mla_w_ukv, mla_qh_norm, mla_kh_norm, w_out_even):
    wi = w_in_even[j]
    d = wi.shape[0]
    o = 0
    seg = {}
    for name, width in (('qa', GLA_QK), ('ka', GLA_QK), ('va', GLA_VW), ('ga', GLA_VW), ('fa', GLA_RANK),
                        ('cq', Q_LORA), ('ckv', KV_LORA), ('kr', MLA_ROPE), ('gb', MLA_VW)):
        seg[name] = wi[:, o:o + width]
        o += width
    z = lambda n: jnp.zeros((d, n), wi.dtype)
    tail = MISC_W - ROPE_LANE0 - MLA_ROPE
    misc = jnp.concatenate([seg['fa'], z(ROPE_LANE0 - GLA_RANK), seg['kr'], z(tail)], 1)
    misc_rot = jnp.concatenate([z(ROPE_LANE0), _rot_half(seg['kr']), z(tail)], 1)
    win = jnp.concatenate([seg['qa'], seg['ka'], seg['va'], 0.5 * seg['ga'], seg['cq'], seg['ckv'],
                           0.5 * seg['gb'], misc, misc_rot], 1)
    wf2 = jnp.pad(gla_w_f2[j], ((0, MISC_W - GLA_RANK), (0, 0)))
    ukv = mla_w_ukv[j].reshape(KV_LORA, MLA_HEADS, MLA_NOPE + MLA_V)
    wuk = ukv[:, :, :MLA_NOPE].reshape(KV_LORA, MLA_NOPEW)
    wuv = ukv[:, :, MLA_NOPE:].reshape(KV_LORA, MLA_VW)
    uq = mla_w_uq[j].reshape(Q_LORA, MLA_HEADS, MLA_QK)
    uq_rot = jnp.concatenate([jnp.zeros_like(uq[..., :MLA_NOPE]), _rot_half(uq[..., MLA_NOPE:])], -1)
    khp = jnp.pad(mla_kh_norm[j], (0, HEAD_PAD - MLA_QK))
    return {
        'ln': ln_even[j][None, :],
        'win': win.astype(BF16),
        'wf2': wf2.astype(BF16),
        'bf': gla_b_f[j][None, :],
        'qn': mla_q_norm[j][None, :],
        'kvn': mla_kv_norm[j][None, :],
        'wuq': _pad_heads(mla_w_uq[j], MLA_QK).astype(BF16),
        'wuqr': _pad_heads(uq_rot.reshape(Q_LORA, MLA_HEADS * MLA_QK), MLA_QK).astype(BF16),
        'wukp': _pad_heads(wuk, MLA_NOPE).astype(BF16),
        'wukt': wuk.T.astype(BF16),
        'wuv': wuv.astype(BF16),
        'wuvt': wuv.T.astype(BF16),
        'qh': jnp.pad(mla_qh_norm[j], (0, HEAD_PAD - MLA_QK))[None, :],
        'qhs': jnp.pad(_swap_half_gain(mla_qh_norm[j]), (0, HEAD_PAD - MLA_QK))[None, :],
        'kh': khp[None, :],
        'khs': jnp.pad(_swap_half_gain(mla_kh_norm[j]), (0, HEAD_PAD - MLA_QK))[None, :],
        'khg': jnp.tile(khp, MLA_HEADS)[None, :],
        'grc': mla_kh_norm[j][MLA_NOPE:][:, None],
        'on': gla_out_norm[j][None, :],
        'wout_a': w_out_even[j][:GLA_VW].astype(BF16),
        'wout_b': w_out_even[j][GLA_VW:].astype(BF16),
    }


def _pack_odd(j, ln_odd, w_in_odd, conv_w, conv_b, rg_w_a, rg_b_a, rg_w_x, rg_b_x, rg_lambda, w_out_odd):
    return {
        'ln': ln_odd[j][None, :],
        'wu': w_in_odd[j][:, :D_RNN].astype(BF16),
        'wg': (0.5 * w_in_odd[j][:, D_RNN:]).astype(BF16),
        'cw': 0.5 * conv_w[j],
        'cb': 0.5 * conv_b[j][None, :],
        'wax': jnp.concatenate([rg_w_a[j], rg_w_x[j]], axis=-1).astype(BF16),
        'ba': 0.5 * rg_b_a[j][None, :],
        'bx': 0.5 * rg_b_x[j][None, :],
        'lam': rg_lambda[j][None, :],
        'wout': w_out_odd[j].astype(BF16),
    }


def _rope_angles(pos):
    inv_freq = ROPE_THETA ** (-jnp.arange(HALF_ROPE, dtype=F32) / HALF_ROPE)
    ang = pos.astype(F32)[:, None] * inv_freq[None, :]
    return jnp.cos(ang), jnp.sin(ang)


def _rope_tile_tables(pos):
    cos, sin = _rope_angles(pos)
    n = pos.shape[0]
    tail = LANES - ROPE_LANE0 - MLA_ROPE
    c = jnp.concatenate([jnp.ones((n, ROPE_LANE0), F32), cos, cos, jnp.ones((n, tail), F32)], 1)
    s = jnp.concatenate([jnp.zeros((n, ROPE_LANE0), F32), sin, sin, jnp.zeros((n, tail), F32)], 1)
    return c, s


def _rope_pair_tables_t(pos):
    cos, sin = _rope_angles(pos)
    return jnp.concatenate([cos, cos], 1).T, jnp.concatenate([sin, sin], 1).T


def _pick(n, prefs):
    for p in prefs:
        if n % p == 0:
            return p
    return n


def kernel(x_prompt, x_sample, state_gla, cache_ckv, cache_kr, state_rglru_h, state_rglru_conv, page_table,
           ln_even, w_in_even, gla_w_f2, gla_b_f, gla_out_norm, mla_q_norm, mla_kv_norm, mla_w_uq, mla_w_ukv,
           mla_qh_norm, mla_kh_norm, w_out_even, ln_odd, w_in_odd, conv_w, conv_b, rg_w_a, rg_b_a, rg_w_x,
           rg_b_x, rg_lambda, w_out_odd):
    bp, tp, d = x_prompt.shape
    bs, ts, _ = x_sample.shape
    n_pages = page_table.shape[1]
    past_len = n_pages * PAGE_SIZE
    depth = ln_even.shape[0] + ln_odd.shape[0]
    assert ts == SUBLANES, "sample group is handled as one 8-row group per sequence"
    assert MLA_NOPE == MLA_V

    tm_p = _pick(tp, (512, 256, 128, 64, 32, 16, 8))
    tm_s = _pick(bs * ts, (256, 128, 64, 32, 16, 8))
    tm_back = _pick(tp, (512, 256, 128, 64, 32, 16, 8))
    nseq_p = _pick(bp, (8, 4, 2, 1))
    nseq_s = _pick(bs, (8, 4, 2, 1))
    chunk_p = min(GLA_CHUNK, tp)
    tblk_p = _pick(tp, (256, 128, 64)) if tp >= GLA_CHUNK else tp
    tq = _pick(tp, (512, 256, 128))
    tt_odd = _pick(tp, (512, 256, 128, 64))
    n_pg = _pick(n_pages, (PAGED_KEYS_PER_STEP // PAGE_SIZE, 32, 16, 8, 4, 2, 1))

    tabs_p = _rope_tile_tables(jnp.arange(tp))
    pos_s = past_len + jnp.arange(ts)
    tabs_s = tuple(jnp.tile(a, (tm_s // ts, 1)) for a in _rope_tile_tables(pos_s))
    pair_new = _rope_pair_tables_t(pos_s)
    pair_past = _rope_pair_tables_t(jnp.arange(past_len))
    pt_flat = page_table.reshape(-1).astype(jnp.int32)
    cache_krt = jnp.swapaxes(cache_kr, -1, -2)

    yp = x_prompt.reshape(bp * tp, d)
    ys = x_sample.reshape(bs * ts, d)
    outs = {k: [] for k in ('gla_p', 'ckv_p', 'kr_p', 'rh_p', 'rc_p', 'gla_s', 'ckv_s', 'kr_s', 'rh_s', 'rc_s')}
    for layer in range(depth):
        j = layer // 2
        if layer % 2 == 0:
            w = _pack_even(j, ln_even, w_in_even, gla_w_f2, gla_b_f, gla_out_norm, mla_q_norm, mla_kv_norm,
                           mla_w_uq, mla_w_ukv, mla_qh_norm, mla_kh_norm, w_out_even)
            qa, ka, va, lf, ga, gb, q, k, v, ckvn, misc = _even_front(yp, w, tabs_p, tm_p, tp)
            r3 = lambda a: a.reshape(bp, tp, a.shape[-1])
            oa, st = _gla_scan(r3(qa), r3(ka), r3(va), r3(lf), None, chunk_p, tblk_p, nseq_p)
            ob = _flash_attn(r3(q), r3(k), v, tq, FLASH_HEADS_PER_STEP)
            yp = _even_back(yp, oa.reshape(bp * tp, GLA_VW), ga, ob.reshape(bp * tp, MLA_VW), gb, w, tm_back)
            outs['gla_p'].append(st.reshape(bp, GLA_HEADS, GLA_DK, GLA_DV))
            outs['ckv_p'].append(ckvn.reshape(bp, tp, KV_LORA))
            outs['kr_p'].append(misc[:, ROPE_LANE0:ROPE_LANE0 + MLA_ROPE].reshape(bp, tp, MLA_ROPE))
            qa, ka, va, lf, ga, gb, q, k, v, ckvn, misc = _even_front(ys, w, tabs_s, tm_s, ts)
            r3 = lambda a: a.reshape(bs, ts, a.shape[-1])
            s0t = state_gla[j].reshape(bs, GLA_HEADS // 2, 2 * GLA_DK, GLA_DV)
            oa, st = _gla_scan(r3(qa), r3(ka), r3(va), r3(lf), s0t, ts, ts, nseq_s)
            krs = misc[:, ROPE_LANE0:ROPE_LANE0 + MLA_ROPE].reshape(bs, ts, MLA_ROPE)
            q4 = q.reshape(bs, ts, MLA_HEADS, HEAD_PAD)
            qrope = jnp.swapaxes(q4[..., ROPE_LANE0:ROPE_LANE0 + MLA_ROPE], 1, 2)
            qrope = qrope.reshape(bs, MLA_HEADS * ts, MLA_ROPE)
            qrr = jnp.concatenate([qrope, qrope[..., HALF_ROPE:], -qrope[..., :HALF_ROPE]], axis=-1)
            ob = _paged_attn(j, pt_flat, n_pages, r3(q), qrr, r3(ckvn), jnp.swapaxes(krs, 1, 2), pair_new,
                             pair_past, w, cache_ckv, cache_krt, n_pg)
            ys = _even_back(ys, oa.reshape(bs * ts, GLA_VW), ga, ob.reshape(bs * ts, MLA_VW), gb, w, tm_s)
            outs['gla_s'].append(st.reshape(bs, GLA_HEADS, GLA_DK, GLA_DV))
            outs['ckv_s'].append(ckvn.reshape(bs, ts, KV_LORA))
            outs['kr_s'].append(krs)
        else:
            w = _pack_odd(j, ln_odd, w_in_odd, conv_w, conv_b, rg_w_a, rg_b_a, rg_w_x, rg_b_x, rg_lambda,
                          w_out_odd)
            y3, hl, cv = _odd_prompt(yp.reshape(bp, tp, d), w, tt_odd)
            yp = y3.reshape(bp * tp, d)
            outs['rh_p'].append(hl.reshape(bp, D_RNN))
            outs['rc_p'].append(cv)
            ext = jnp.concatenate([jnp.zeros((bs, ts - CONV_W, D_RNN), F32), state_rglru_h[j][:, None, :],
                                   state_rglru_conv[j]], axis=1).reshape(bs * ts, D_RNN)
            ys, hfull, ufull = _odd_sample(ys, ext, w, tm_s)
            outs['rh_s'].append(hfull.reshape(bs, ts, D_RNN)[:, ts - 1])
            outs['rc_s'].append(ufull.reshape(bs, ts, D_RNN)[:, ts - (CONV_W - 1):])
    st_ = lambda name: jnp.stack(outs[name])
    return (yp.reshape(bp, tp, d), ys.reshape(bs, ts, d),
            st_('gla_p'), st_('ckv_p'), st_('kr_p'), st_('rh_p'), st_('rc_p'),
            st_('gla_s'), st_('ckv_s'), st_('kr_s'), st_('rh_s'), st_('rc_s'))
```
